```python
import math
import jax, jax.numpy as jnp
from jax import lax
import numpy as np

D_MODEL = 2048
BATCH = 2
SEQ = 4096
DEPTH = 2
DEC_BATCH = 32
DEC_SEQ = 64
PAST_LEN = 2048

CHUNK = 64
W_LRU = 1024
LRU_BLOCKS = 16
LRU_CONV = 4
LRU_C = 8.0
W_CONF = 1024
CONF_CONV = 31
N_HEADS = 8
N_KV = 2
HEAD_DIM = 128
ROT_DIM = HEAD_DIM // 4
N_IDX_HEADS = 8
IDX_DIM = 64
IDX_ROT = IDX_DIM // 4
TOPK_ATT = 256
ROPE_THETA = 500000.0
Q_BLOCK = 128
W_POOL = 1024
POOL_WINDOWS = (2, 4, 8, 16)
POOL_GROUPS = 4
POOL_GW = W_POOL // POOL_GROUPS
POOL_BUF = 15
N_BRANCH = 4
W_BRANCH = 1024
PEER_HEADS = 8
PEER_NKEYS = 128
PEER_EXPERTS = PEER_NKEYS * PEER_NKEYS
PEER_QDIM = 256
PEER_HALF = PEER_QDIM // 2
PEER_TOPK = 16
PEER_BLOCK = 128
ALPHA = (2 * DEPTH) ** 0.25
BETA = (8 * DEPTH) ** -0.25
LN_EPS = 1e-5
PROJ_SIZES = (W_LRU, W_LRU, 2 * W_CONF, N_HEADS * HEAD_DIM, N_KV * HEAD_DIM, N_KV * HEAD_DIM,
              N_IDX_HEADS * IDX_DIM, IDX_DIM, N_IDX_HEADS, W_POOL, N_BRANCH * D_MODEL)
PROJ_TOTAL = 15432

kernel_name = 'hybrid_stream_rglru_conformer_dsa_pool_peer_step'


def _layer_norm(x, g, b):
    xf = x.astype(jnp.float32)
    mu = jnp.mean(xf, -1, keepdims=True)
    var = jnp.mean(jnp.square(xf - mu), -1, keepdims=True)
    y = (xf - mu) * lax.rsqrt(var + LN_EPS) * g.astype(jnp.float32) + b.astype(jnp.float32)
    return y.astype(x.dtype)


def _rope_partial(x, pos, rot):
    half = rot // 2
    inv = ROPE_THETA ** (-jnp.arange(half, dtype=jnp.float32) / half)
    ang = pos.astype(jnp.float32)[:, None] * inv[None, :]
    cos = jnp.cos(ang)[None, :, None, :]
    sin = jnp.sin(ang)[None, :, None, :]
    xr = x[..., :rot].astype(jnp.float32)
    x1, x2 = xr[..., :half], xr[..., half:]
    rot_part = jnp.concatenate([x1 * cos - x2 * sin, x2 * cos + x1 * sin], -1)
    return jnp.concatenate([rot_part.astype(x.dtype), x[..., rot:]], -1)


def _causal_dwconv(buf, x, w, b):
    t = x.shape[1]
    xp = jnp.concatenate([buf.astype(x.dtype), x], axis=1)
    y = lax.conv_general_dilated(xp, w.astype(x.dtype)[:, None, :], (1,), 'VALID',
                                 dimension_numbers=('NWC', 'WIO', 'NWC'),
                                 feature_group_count=x.shape[-1])
    return y + b.astype(x.dtype), xp[:, t:]


def _lin_combine(left, right):
    a_l, b_l = left
    a_r, b_r = right
    return a_l * a_r, a_r * b_l + b_r


def _rglru(xc, h0, w_r, b_r, w_i, b_i, lam):
    bsz, t, c = xc.shape
    xb = xc.reshape(bsz, t, LRU_BLOCKS, c // LRU_BLOCKS)
    gr = jnp.einsum('btnc,ncd->btnd', xb, w_r).reshape(bsz, t, c) + b_r
    gi = jnp.einsum('btnc,ncd->btnd', xb, w_i).reshape(bsz, t, c) + b_i
    r = jax.nn.sigmoid(gr.astype(jnp.float32))
    i = jax.nn.sigmoid(gi.astype(jnp.float32))
    log_a = -LRU_C * r * jax.nn.softplus(-lam.astype(jnp.float32))
    a = jnp.exp(log_a)
    u = jnp.sqrt(-jnp.expm1(2.0 * log_a)) * i * xc.astype(jnp.float32)
    u = u.at[:, 0].add(a[:, 0] * h0.astype(jnp.float32))
    _, h = lax.associative_scan(_lin_combine, (a, u), axis=1)
    return h, h[:, -1]


def _pool_mix(buf, xin, w_pool, scale):
    bsz, t, c = xin.shape
    xp = jnp.concatenate([buf.astype(xin.dtype), xin], axis=1)
    xf = xp.astype(jnp.float32)
    cs = jnp.concatenate([jnp.zeros((bsz, 1, c), jnp.float32), jnp.cumsum(xf, axis=1)], axis=1)
    means = []
    for g, w in enumerate(POOL_WINDOWS):
        lo = g * POOL_GW
        hi = lo + POOL_GW
        win_sum = cs[:, POOL_BUF + 1:, lo:hi] - cs[:, POOL_BUF + 1 - w:POOL_BUF + 1 - w + t, lo:hi]
        means.append(win_sum * (1.0 / w))
    diff = jnp.concatenate(means, -1) - xf[:, POOL_BUF:]
    diff = diff.astype(xin.dtype).reshape(bsz, t, POOL_GROUPS, POOL_GW)
    y = jnp.einsum('btgc,gcd->btgd', diff, w_pool).reshape(bsz, t, c) * scale
    return y, xp[:, t:]


def _dsa_block(q, qi, wi, q_pos, k_all, v_all, ki_all, n_top):
    bsz, tq = q.shape[0], q.shape[1]
    n_keys = k_all.shape[1]
    limit = (q_pos // CHUNK + 1) * CHUNK
    adm = jnp.arange(n_keys, dtype=jnp.int32)[None, :] < limit[:, None]
    dots = jnp.einsum('bqhd,bsd->bqhs', qi, ki_all).astype(jnp.float32)
    score = jnp.einsum('bqh,bqhs->bqs', wi.astype(jnp.float32), jax.nn.relu(dots))
    score = jnp.where(adm[None], score, -jnp.inf)
    _, idx = lax.top_k(score, n_top)
    valid = idx < limit[None, :, None]
    gather = jax.vmap(lambda rows, ids: rows[ids])
    kg = gather(k_all, idx)
    vg = gather(v_all, idx)
    qg = q.reshape(bsz, tq, N_KV, N_HEADS // N_KV, HEAD_DIM)
    s = jnp.einsum('bqgrd,bqkgd->bqgrk', qg, kg).astype(jnp.float32) * (HEAD_DIM ** -0.5)
    s = jnp.where(valid[:, :, None, None, :], s, -jnp.inf)
    p = jax.nn.softmax(s, axis=-1).astype(vg.dtype)
    o = jnp.einsum('bqgrk,bqkgd->bqgrd', p, vg)
    return o.reshape(bsz, tq, N_HEADS * HEAD_DIM)


def _dsa_attention(q, qi, wi, pos, k_all, v_all, ki_all):
    bsz, t = q.shape[0], q.shape[1]
    n_top = min(TOPK_ATT, k_all.shape[1] // 4)
    if t > Q_BLOCK and t % Q_BLOCK == 0:
        nb = t // Q_BLOCK

        def to_blocks(a):
            return jnp.moveaxis(a.reshape((bsz, nb, Q_BLOCK) + a.shape[2:]), 1, 0)

        def one_block(args):
            qb, qib, wib, pb = args
            return _dsa_block(qb, qib, wib, pb, k_all, v_all, ki_all, n_top)

        out = lax.map(one_block, (to_blocks(q), to_blocks(qi), to_blocks(wi), pos.reshape(nb, Q_BLOCK)))
        return jnp.moveaxis(out, 0, 1).reshape(bsz, t, N_HEADS * HEAD_DIM)
    return _dsa_block(q, qi, wi, pos, k_all, v_all, ki_all, n_top)


def _peer(x, w_q, sub_keys, u_tab, v_tab):
    bsz, t, d = x.shape
    n = bsz * t
    nb = -(-n // PEER_BLOCK)
    xt = jnp.pad(x.reshape(n, d), ((0, nb * PEER_BLOCK - n), (0, 0))).reshape(nb, PEER_BLOCK, d)

    def one_block(xb):
        q = (xb @ w_q).reshape(PEER_BLOCK, PEER_HEADS, 2, PEER_HALF)
        s = jnp.einsum('thcd,hcnd->thcn', q, sub_keys).astype(jnp.float32)
        sv, si = lax.top_k(s, PEER_TOPK)
        cand = sv[:, :, 0, :, None] + sv[:, :, 1, None, :]
        fv, fi = lax.top_k(cand.reshape(PEER_BLOCK, PEER_HEADS, PEER_TOPK * PEER_TOPK), PEER_TOPK)
        i1 = jnp.take_along_axis(si[:, :, 0], fi // PEER_TOPK, axis=-1)
        i2 = jnp.take_along_axis(si[:, :, 1], fi % PEER_TOPK, axis=-1)
        eid = i1 * PEER_NKEYS + i2
        g = jax.nn.softmax(fv, axis=-1)
        ug = u_tab[eid]
        vg = v_tab[eid]
        act = jnp.einsum('td,thkd->thk', xb, ug).astype(jnp.float32)
        coef = (g * jax.nn.gelu(act)).astype(xb.dtype)
        return jnp.einsum('thk,thkd->td', coef, vg)

    y = lax.map(one_block, xt).reshape(nb * PEER_BLOCK, d)[:n]
    return y.reshape(bsz, t, d)


def _trunk(x, pos0, past_k, past_v, past_ki, h0, lru_buf, conf_buf, pool_buf, prm):
    bsz, t, _ = x.shape
    pos = pos0 + jnp.arange(t, dtype=jnp.int32)
    offs = [int(o) for o in np.cumsum(PROJ_SIZES)[:-1]]
    ks, vs, kis, hs, lbs, cbs, pbs = [], [], [], [], [], [], []
    for l in range(DEPTH):
        z = jnp.einsum('btd,de->bte', x, prm['w_in'][l])
        za, zg, zb, zq, zk, zv, zqi, zki, zwi, zp, zgate = jnp.split(z, offs, axis=-1)
        xc, lru_new = _causal_dwconv(lru_buf[l], za, prm['lru_conv_w'][l], prm['lru_conv_b'][l])
        h, h_last = _rglru(xc, h0[l], prm['lru_wr'][l], prm['lru_br'][l], prm['lru_wi'][l],
                           prm['lru_bi'][l], prm['lru_lambda'][l])
        y_a = h.astype(x.dtype) * jax.nn.gelu(zg)
        glu = zb[..., :W_CONF] * jax.nn.sigmoid(zb[..., W_CONF:])
        cb, conf_new = _causal_dwconv(conf_buf[l], glu, prm['conf_conv_w'][l], prm['conf_conv_b'][l])
        y_b = jax.nn.silu(_layer_norm(cb, prm['conf_ln_g'][l], prm['conf_ln_b'][l]))
        q = _rope_partial(zq.reshape(bsz, t, N_HEADS, HEAD_DIM), pos, ROT_DIM)
        k = _rope_partial(zk.reshape(bsz, t, N_KV, HEAD_DIM), pos, ROT_DIM)
        v = zv.reshape(bsz, t, N_KV, HEAD_DIM)
        qi = _rope_partial(zqi.reshape(bsz, t, N_IDX_HEADS, IDX_DIM), pos, IDX_ROT)
        ki = _rope_partial(zki[:, :, None, :], pos, IDX_ROT)[:, :, 0, :]
        k_all = jnp.concatenate([past_k[l].astype(x.dtype), k], axis=1)
        v_all = jnp.concatenate([past_v[l].astype(x.dtype), v], axis=1)
        ki_all = jnp.concatenate([past_ki[l].astype(x.dtype), ki], axis=1)
        y_c = _dsa_attention(q, qi, zwi, pos, k_all, v_all, ki_all)
        y_d, pool_new = _pool_mix(pool_buf[l], zp, prm['pool_w'][l], prm['pool_scale'][l])
        branches = jnp.stack([y_a, y_b, y_c, y_d], axis=2)
        proj = jnp.einsum('btnw,nwd->btnd', branches, prm['w_branch'][l])
        gates = jax.nn.sigmoid(zgate.reshape(bsz, t, N_BRANCH, D_MODEL))
        mix = jnp.sum(gates * proj, axis=2) @ prm['w_out'][l]
        x = _layer_norm(ALPHA * x + mix, prm['ln1_g'][l], prm['ln1_b'][l])
        ffn = _peer(x, prm['peer_wq'][l], prm['peer_subkeys'][l], prm['peer_u'][l], prm['peer_v'][l])
        x = _layer_norm(ALPHA * x + ffn, prm['ln2_g'][l], prm['ln2_b'][l])
        ks.append(k)
        vs.append(v)
        kis.append(ki)
        hs.append(h_last)
        lbs.append(lru_new)
        cbs.append(conf_new)
        pbs.append(pool_new)
    states = (jnp.stack(ks), jnp.stack(vs), jnp.stack(kis), jnp.stack(hs),
              jnp.stack(lbs), jnp.stack(cbs), jnp.stack(pbs))
    return x, states


def setup_inputs(seed: int = 0) -> dict:
    key = jax.random.key(seed)
    ks = jax.random.split(key, 33)

    def nrm(i, shape, scale):
        return jax.random.normal(ks[i], shape, jnp.float32) * scale

    a0 = jax.random.uniform(ks[17], (DEPTH, W_LRU), jnp.float32, 0.9, 0.999)
    return {
        'x_prompt': nrm(0, (BATCH, SEQ, D_MODEL), 1.0),
        'x_sample': nrm(1, (DEC_BATCH, DEC_SEQ, D_MODEL), 1.0),
        'cache_k': nrm(2, (DEPTH, DEC_BATCH, PAST_LEN, N_KV, HEAD_DIM), 1.0),
        'cache_v': nrm(3, (DEPTH, DEC_BATCH, PAST_LEN, N_KV, HEAD_DIM), 1.0),
        'cache_kidx': nrm(4, (DEPTH, DEC_BATCH, PAST_LEN, IDX_DIM), 1.0),
        'state_lru_h': nrm(5, (DEPTH, DEC_BATCH, W_LRU), 0.5),
        'state_lru_conv': nrm(6, (DEPTH, DEC_BATCH, LRU_CONV - 1, W_LRU), 1.0),
        'state_conf_conv': nrm(7, (DEPTH, DEC_BATCH, CONF_CONV - 1, W_CONF), 1.0),
        'state_pool': nrm(8, (DEPTH, DEC_BATCH, POOL_BUF, W_POOL), 1.0),
        'w_in': nrm(9, (DEPTH, D_MODEL, PROJ_TOTAL), D_MODEL ** -0.5),
        'lru_conv_w': nrm(10, (DEPTH, LRU_CONV, W_LRU), LRU_CONV ** -0.5),
        'lru_conv_b': nrm(11, (DEPTH, W_LRU), 0.01),
        'lru_wr': nrm(12, (DEPTH, LRU_BLOCKS, W_LRU // LRU_BLOCKS, W_LRU // LRU_BLOCKS), (W_LRU // LRU_BLOCKS) ** -0.5),
        'lru_br': nrm(13, (DEPTH, W_LRU), 0.01),
        'lru_wi': nrm(14, (DEPTH, LRU_BLOCKS, W_LRU // LRU_BLOCKS, W_LRU // LRU_BLOCKS), (W_LRU // LRU_BLOCKS) ** -0.5),
        'lru_bi': nrm(15, (DEPTH, W_LRU), 0.01),
        'lru_lambda': jnp.log(a0) - jnp.log1p(-a0) + nrm(16, (DEPTH, W_LRU), 0.01),
        'conf_conv_w': nrm(18, (DEPTH, CONF_CONV, W_CONF), CONF_CONV ** -0.5),
        'conf_conv_b': nrm(19, (DEPTH, W_CONF), 0.01),
        'conf_ln_g': 1.0 + nrm(20, (DEPTH, W_CONF), 0.05),
        'conf_ln_b': nrm(21, (DEPTH, W_CONF), 0.01),
        'pool_w': nrm(22, (DEPTH, POOL_GROUPS, POOL_GW, POOL_GW), POOL_GW ** -0.5),
        'pool_scale': 1.0 + nrm(23, (DEPTH, W_POOL), 0.1),
        'w_branch': nrm(24, (DEPTH, N_BRANCH, W_BRANCH, D_MODEL), W_BRANCH ** -0.5),
        'w_out': nrm(25, (DEPTH, D_MODEL, D_MODEL), BETA * D_MODEL ** -0.5),
        'ln1_g': 1.0 + nrm(26, (DEPTH, D_MODEL), 0.05),
        'ln1_b': nrm(27, (DEPTH, D_MODEL), 0.01),
        'peer_wq': nrm(28, (DEPTH, D_MODEL, PEER_HEADS * PEER_QDIM), D_MODEL ** -0.5),
        'peer_subkeys': nrm(29, (DEPTH, PEER_HEADS, 2, PEER_NKEYS, PEER_HALF), PEER_HALF ** -0.5),
        'peer_u': nrm(30, (DEPTH, PEER_EXPERTS, D_MODEL), D_MODEL ** -0.5),
        'peer_v': nrm(31, (DEPTH, PEER_EXPERTS, D_MODEL), BETA * PEER_HEADS ** -0.5),
        'ln2_g': 1.0 + nrm(32, (DEPTH, D_MODEL), 0.05),
        'ln2_b': nrm(26, (DEPTH, D_MODEL), 0.01) * 0.5 + nrm(27, (DEPTH, D_MODEL), 0.01),
    }


def reference(x_prompt, x_sample, cache_k, cache_v, cache_kidx, state_lru_h, state_lru_conv,
              state_conf_conv, state_pool, w_in, lru_conv_w, lru_conv_b, lru_wr, lru_br, lru_wi,
              lru_bi, lru_lambda, conf_conv_w, conf_conv_b, conf_ln_g, conf_ln_b, pool_w, pool_scale,
              w_branch, w_out, ln1_g, ln1_b, peer_wq, peer_subkeys, peer_u, peer_v, ln2_g, ln2_b):
    prm = {
        'w_in': w_in, 'lru_conv_w': lru_conv_w, 'lru_conv_b': lru_conv_b, 'lru_wr': lru_wr,
        'lru_br': lru_br, 'lru_wi': lru_wi, 'lru_bi': lru_bi, 'lru_lambda': lru_lambda,
        'conf_conv_w': conf_conv_w, 'conf_conv_b': conf_conv_b, 'conf_ln_g': conf_ln_g,
        'conf_ln_b': conf_ln_b, 'pool_w': pool_w, 'pool_scale': pool_scale, 'w_branch': w_branch,
        'w_out': w_out, 'ln1_g': ln1_g, 'ln1_b': ln1_b, 'peer_wq': peer_wq,
        'peer_subkeys': peer_subkeys, 'peer_u': peer_u, 'peer_v': peer_v, 'ln2_g': ln2_g, 'ln2_b': ln2_b,
    }
    bp = x_prompt.shape[0]
    dt = x_prompt.dtype
    y_prompt, st_p = _trunk(
        x_prompt, 0,
        jnp.zeros((DEPTH, bp, 0, N_KV, HEAD_DIM), dt),
        jnp.zeros((DEPTH, bp, 0, N_KV, HEAD_DIM), dt),
        jnp.zeros((DEPTH, bp, 0, IDX_DIM), dt),
        jnp.zeros((DEPTH, bp, W_LRU), jnp.float32),
        jnp.zeros((DEPTH, bp, LRU_CONV - 1, W_LRU), dt),
        jnp.zeros((DEPTH, bp, CONF_CONV - 1, W_CONF), dt),
        jnp.zeros((DEPTH, bp, POOL_BUF, W_POOL), dt),
        prm)
    k_p, v_p, kidx_p, h_p, lconv_p, cconv_p, pool_p = st_p
    y_sample, st_s = _trunk(
        x_sample, cache_k.shape[2], cache_k, cache_v, cache_kidx, state_lru_h,
        state_lru_conv, state_conf_conv, state_pool, prm)
    k_s, v_s, kidx_s, h_s, lconv_s, cconv_s, pool_s = st_s
    return (y_prompt, y_sample, k_p, v_p, kidx_p, h_p, lconv_p, cconv_p, pool_p,
            k_s, v_s, kidx_s, h_s, lconv_s, cconv_s, pool_s)
```

```python
import functools
import math
import jax, jax.numpy as jnp
from jax import lax
import numpy as np
from jax.experimental import pallas as pl
from jax.experimental.pallas import tpu as pltpu

D_MODEL = 2048
BATCH = 2
SEQ = 4096
DEPTH = 2
DEC_BATCH = 32
DEC_SEQ = 64
PAST_LEN = 2048

CHUNK = 64
W_LRU = 1024
LRU_BLOCKS = 16
LRU_CONV = 4
LRU_C = 8.0
W_CONF = 1024
CONF_CONV = 31
N_HEADS = 8
N_KV = 2
HEAD_DIM = 128
ROT_DIM = HEAD_DIM // 4
N_IDX_HEADS = 8
IDX_DIM = 64
IDX_ROT = IDX_DIM // 4
TOPK_ATT = 256
ROPE_THETA = 500000.0
Q_BLOCK = 128
W_POOL = 1024
POOL_WINDOWS = (2, 4, 8, 16)
POOL_GROUPS = 4
POOL_GW = W_POOL // POOL_GROUPS
POOL_BUF = 15
N_BRANCH = 4
W_BRANCH = 1024
PEER_HEADS = 8
PEER_NKEYS = 128
PEER_TOPK = 16
PEER_EXPERTS = PEER_NKEYS * PEER_NKEYS
PEER_QDIM = 256
PEER_HALF = PEER_QDIM // 2
PEER_BLOCK = 128
ALPHA = (2 * DEPTH) ** 0.25
BETA = (8 * DEPTH) ** -0.25
LN_EPS = 1e-5
PROJ_SIZES = (W_LRU, W_LRU, 2 * W_CONF, N_HEADS * HEAD_DIM, N_KV * HEAD_DIM, N_KV * HEAD_DIM,
              N_IDX_HEADS * IDX_DIM, IDX_DIM, N_IDX_HEADS, W_POOL, N_BRANCH * D_MODEL)
PROJ_TOTAL = 15432

F32 = jnp.float32
BF16 = jnp.bfloat16
VMEM_LIMIT_BYTES = 56 * 1024 * 1024
LANES = 128
GELU_C = math.sqrt(2.0 / math.pi)


def _cparams(*sem):
    return pltpu.CompilerParams(dimension_semantics=sem, vmem_limit_bytes=VMEM_LIMIT_BYTES)


def _peer_scores_body(x_ref, wq_ref, sk_ref, s_ref, xt_ref, *, n_hc):
    x = x_ref[...]
    xt_ref[...] = x.T.astype(BF16)
    q = jnp.dot(x.astype(BF16), wq_ref[...], preferred_element_type=F32)
    for hc in range(n_hc):
        qh = q[:, hc * LANES:(hc + 1) * LANES].astype(BF16)
        s_ref[hc] = lax.dot_general(sk_ref[hc], qh, (((1,), (1,)), ((), ())), preferred_element_type=F32)


def peer_scores(x, wq_bf, sk_bf, tb):
    m, d = x.shape
    n_hc = sk_bf.shape[0]
    return pl.pallas_call(
        functools.partial(_peer_scores_body, n_hc=n_hc),
        grid=(m // tb,),
        in_specs=[pl.BlockSpec((tb, d), lambda i: (i, 0)),
                  pl.BlockSpec((d, n_hc * LANES), lambda i: (0, 0)),
                  pl.BlockSpec((n_hc, LANES, LANES), lambda i: (0, 0, 0))],
        out_specs=[pl.BlockSpec((n_hc, LANES, tb), lambda i: (0, 0, i)),
                   pl.BlockSpec((d, tb), lambda i: (0, i))],
        out_shape=[jax.ShapeDtypeStruct((n_hc, LANES, m), F32),
                   jax.ShapeDtypeStruct((d, m), BF16)],
        compiler_params=_cparams("arbitrary"),
    )(x, wq_bf, sk_bf)


N_EXTRACT = PEER_TOPK + 1


def _top_desc(s, n):
    rows = lax.broadcasted_iota(jnp.int32, s.shape, 0)
    rem = s
    out = []
    for k in range(n):
        mx = jnp.max(rem, axis=0, keepdims=True)
        out.append(mx)
        if k + 1 < n:
            first = jnp.min(jnp.where(rem == mx, rows, PEER_NKEYS), axis=0, keepdims=True)
            rem = jnp.where(rows == first, -jnp.inf, rem)
    return out


def _stack_rows(vals, nrows, tl):
    rows = lax.broadcasted_iota(jnp.int32, (nrows, tl), 0)
    acc = jnp.full((nrows, tl), -jnp.inf, F32)
    for k, v in enumerate(vals):
        acc = jnp.where(rows == k, v, acc)
    return acc


def _peer_select_body(s_ref, a_ref, b_ref, c_ref, *, n_heads, tl):
    n = N_EXTRACT
    rows8 = lax.broadcasted_iota(jnp.int32, (8, tl), 0)
    for h in range(n_heads):
        s1 = s_ref[2 * h]
        s2 = s_ref[2 * h + 1]
        sv1 = _top_desc(s1, n)
        sv2 = _top_desc(s2, n)
        pieces = [_stack_rows(sv2, 24, tl) + sv1[0]]
        sv2_8 = _stack_rows(sv2[:8], 8, tl)
        for i in range(1, 8):
            nj = n // (i + 1)
            pieces.append(jnp.where(rows8 < nj, sv2_8 + sv1[i], -jnp.inf))
        pieces.append(_stack_rows(sv1[8:], 16, tl) + sv2[0])
        cand = jnp.concatenate(pieces, axis=0)
        taken = jnp.zeros((1, tl), F32)
        v16 = jnp.zeros((1, tl), F32)
        v17 = jnp.zeros((1, tl), F32)
        rem = cand
        for k in range(n):
            mx = jnp.max(rem, axis=0, keepdims=True)
            eq = rem == mx
            cnt = jnp.sum(jnp.where(eq, 1.0, 0.0), axis=0, keepdims=True)
            new_taken = taken + cnt
            v16 = jnp.where((taken < PEER_TOPK) & (new_taken >= PEER_TOPK), mx, v16)
            v17 = jnp.where((taken < n) & (new_taken >= n), mx, v17)
            rem = jnp.where(eq, -jnp.inf, rem)
            taken = new_taken
        thr = 0.5 * (v16 + v17)
        top = sv1[0] + sv2[0]
        z = jnp.sum(jnp.where(cand >= v16, jnp.exp(cand - top), 0.0), axis=0, keepdims=True)
        a_ref[h] = jnp.exp(s1 - sv1[0]) / z
        b_ref[h] = jnp.exp(s2 - sv2[0])
        c_ref[h] = thr - s1


def peer_select(s, tl):
    n_hc, _, m = s.shape
    n_heads = n_hc // 2
    spec = pl.BlockSpec((n_heads, PEER_NKEYS, tl), lambda i: (0, 0, i))
    shp = jax.ShapeDtypeStruct((n_heads, PEER_NKEYS, m), F32)
    return pl.pallas_call(
        functools.partial(_peer_select_body, n_heads=n_heads, tl=tl),
        grid=(m // tl,),
        in_specs=[pl.BlockSpec((n_hc, PEER_NKEYS, tl), lambda i: (0, 0, i))],
        out_specs=[spec, spec, spec],
        out_shape=[shp, shp, shp],
        compiler_params=_cparams("arbitrary"),
    )(s)


def _gelu_tanh(x):
    return 0.5 * x * (1.0 + jnp.tanh(GELU_C * (x + 0.044715 * (x * x * x))))


def _peer_dense_body(xt_ref, u_ref, vt_ref, a_ref, b_ref, c_ref, s2_ref, o_ref, acc_ref, act_ref, coef_ref,
                     *, n_heads, eb, tbt, lc):
    j = pl.program_id(1)

    @pl.when(j == 0)
    def _():
        acc_ref[...] = jnp.zeros_like(acc_ref)

    act_ref[...] = jnp.dot(u_ref[...], xt_ref[...], preferred_element_type=F32)
    nr = eb // PEER_NKEYS

    def lane_chunk(ci, carry):
        l0 = pl.multiple_of(ci * lc, lc)
        for rr in range(nr):
            r = j * nr + rr
            w = jnp.zeros((PEER_NKEYS, lc), F32)
            for h in range(n_heads):
                a = a_ref[h, pl.ds(r, 1), pl.ds(l0, lc)]
                c = c_ref[h, pl.ds(r, 1), pl.ds(l0, lc)]
                s2 = s2_ref[h, :, pl.ds(l0, lc)]
                b = b_ref[h, :, pl.ds(l0, lc)]
                w = w + jnp.where(s2 >= c, a * b, 0.0)
            x = act_ref[rr * PEER_NKEYS:(rr + 1) * PEER_NKEYS, pl.ds(l0, lc)]
            coef_ref[rr * PEER_NKEYS:(rr + 1) * PEER_NKEYS, pl.ds(l0, lc)] = (w * _gelu_tanh(x)).astype(BF16)
        return carry

    lax.fori_loop(0, tbt // lc, lane_chunk, 0)
    acc_ref[...] += jnp.dot(vt_ref[...], coef_ref[...], preferred_element_type=F32)

    @pl.when(j == pl.num_programs(1) - 1)
    def _():
        o_ref[...] = acc_ref[...].T


def peer_dense(xt_bf, u_bf, vt_bf, a, b, c, s, tbt, eb, lc=256):
    d, m = xt_bf.shape
    e = u_bf.shape[0]
    n_heads = a.shape[0]
    s4 = s.reshape(n_heads, 2, PEER_NKEYS, m)
    hspec = pl.BlockSpec((n_heads, PEER_NKEYS, tbt), lambda i, j: (0, 0, i))
    return pl.pallas_call(
        functools.partial(_peer_dense_body, n_heads=n_heads, eb=eb, tbt=tbt, lc=lc),
        grid=(m // tbt, e // eb),
        in_specs=[pl.BlockSpec((d, tbt), lambda i, j: (0, i)),
                  pl.BlockSpec((eb, d), lambda i, j: (j, 0)),
                  pl.BlockSpec((d, eb), lambda i, j: (0, j)),
                  hspec, hspec, hspec,
                  pl.BlockSpec((n_heads, None, PEER_NKEYS, tbt), lambda i, j: (0, 1, 0, i))],
        out_specs=pl.BlockSpec((tbt, d), lambda i, j: (i, 0)),
        out_shape=jax.ShapeDtypeStruct((m, d), F32),
        scratch_shapes=[pltpu.VMEM((d, tbt), F32), pltpu.VMEM((eb, tbt), F32), pltpu.VMEM((eb, tbt), BF16)],
        compiler_params=_cparams("arbitrary", "arbitrary"),
    )(xt_bf, u_bf, vt_bf, a, b, c, s4)


def peer_pallas(x, wq, sub_keys, u_tab, v_tab, tb=512, tl=256, tbt=512, eb=512):
    n_heads = sub_keys.shape[0]
    sk_bf = sub_keys.reshape(n_heads * 2, PEER_NKEYS, -1).astype(BF16)
    s, xt = peer_scores(x, wq.astype(BF16), sk_bf, tb)
    a, b, c = peer_select(s, tl)
    return peer_dense(xt, u_tab.astype(BF16), v_tab.T.astype(BF16), a, b, c, s, tbt, eb)


def _layer_norm(x, g, b):
    xf = x.astype(jnp.float32)
    mu = jnp.mean(xf, -1, keepdims=True)
    var = jnp.mean(jnp.square(xf - mu), -1, keepdims=True)
    y = (xf - mu) * lax.rsqrt(var + LN_EPS) * g.astype(jnp.float32) + b.astype(jnp.float32)
    return y.astype(x.dtype)


def _rope_partial(x, pos, rot):
    half = rot // 2
    inv = ROPE_THETA ** (-jnp.arange(half, dtype=jnp.float32) / half)
    ang = pos.astype(jnp.float32)[:, None] * inv[None, :]
    cos = jnp.cos(ang)[None, :, None, :]
    sin = jnp.sin(ang)[None, :, None, :]
    xr = x[..., :rot].astype(jnp.float32)
    x1, x2 = xr[..., :half], xr[..., half:]
    rot_part = jnp.concatenate([x1 * cos - x2 * sin, x2 * cos + x1 * sin], -1)
    return jnp.concatenate([rot_part.astype(x.dtype), x[..., rot:]], -1)


def _causal_dwconv(buf, x, w, b):
    t = x.shape[1]
    xp = jnp.concatenate([buf.astype(x.dtype), x], axis=1)
    y = lax.conv_general_dilated(xp, w.astype(x.dtype)[:, None, :], (1,), 'VALID',
                                 dimension_numbers=('NWC', 'WIO', 'NWC'),
                                 feature_group_count=x.shape[-1])
    return y + b.astype(x.dtype), xp[:, t:]


def _lin_combine(left, right):
    a_l, b_l = left
    a_r, b_r = right
    return a_l * a_r, a_r * b_l + b_r


def _rglru(xc, h0, w_r, b_r, w_i, b_i, lam):
    bsz, t, c = xc.shape
    xb = xc.reshape(bsz, t, LRU_BLOCKS, c // LRU_BLOCKS)
    gr = jnp.einsum('btnc,ncd->btnd', xb, w_r).reshape(bsz, t, c) + b_r
    gi = jnp.einsum('btnc,ncd->btnd', xb, w_i).reshape(bsz, t, c) + b_i
    r = jax.nn.sigmoid(gr.astype(jnp.float32))
    i = jax.nn.sigmoid(gi.astype(jnp.float32))
    log_a = -LRU_C * r * jax.nn.softplus(-lam.astype(jnp.float32))
    a = jnp.exp(log_a)
    u = jnp.sqrt(-jnp.expm1(2.0 * log_a)) * i * xc.astype(jnp.float32)
    u = u.at[:, 0].add(a[:, 0] * h0.astype(jnp.float32))
    _, h = lax.associative_scan(_lin_combine, (a, u), axis=1)
    return h, h[:, -1]


def _pool_mix(buf, xin, w_pool, scale):
    bsz, t, c = xin.shape
    xp = jnp.concatenate([buf.astype(xin.dtype), xin], axis=1)
    xf = xp.astype(jnp.float32)
    cs = jnp.concatenate([jnp.zeros((bsz, 1, c), jnp.float32), jnp.cumsum(xf, axis=1)], axis=1)
    means = []
    for g, w in enumerate(POOL_WINDOWS):
        lo = g * POOL_GW
        hi = lo + POOL_GW
        win_sum = cs[:, POOL_BUF + 1:, lo:hi] - cs[:, POOL_BUF + 1 - w:POOL_BUF + 1 - w + t, lo:hi]
        means.append(win_sum * (1.0 / w))
    diff = jnp.concatenate(means, -1) - xf[:, POOL_BUF:]
    diff = diff.astype(xin.dtype).reshape(bsz, t, POOL_GROUPS, POOL_GW)
    y = jnp.einsum('btgc,gcd->btgd', diff, w_pool).reshape(bsz, t, c) * scale
    return y, xp[:, t:]


def _dsa_block(q, qi, wi, q_pos, k_all, v_all, ki_all, n_top):
    bsz, tq = q.shape[0], q.shape[1]
    n_keys = k_all.shape[1]
    limit = (q_pos // CHUNK + 1) * CHUNK
    adm = jnp.arange(n_keys, dtype=jnp.int32)[None, :] < limit[:, None]
    dots = jnp.einsum('bqhd,bsd->bqhs', qi, ki_all).astype(jnp.float32)
    score = jnp.einsum('bqh,bqhs->bqs', wi.astype(jnp.float32), jax.nn.relu(dots))
    score = jnp.where(adm[None], score, -jnp.inf)
    _, idx = lax.top_k(score, n_top)
    valid = idx < limit[None, :, None]
    gather = jax.vmap(lambda rows, ids: rows[ids])
    kg = gather(k_all, idx)
    vg = gather(v_all, idx)
    qg = q.reshape(bsz, tq, N_KV, N_HEADS // N_KV, HEAD_DIM)
    s = jnp.einsum('bqgrd,bqkgd->bqgrk', qg, kg).astype(jnp.float32) * (HEAD_DIM ** -0.5)
    s = jnp.where(valid[:, :, None, None, :], s, -jnp.inf)
    p = jax.nn.softmax(s, axis=-1).astype(vg.dtype)
    o = jnp.einsum('bqgrk,bqkgd->bqgrd', p, vg)
    return o.reshape(bsz, tq, N_HEADS * HEAD_DIM)


def _dsa_attention(q, qi, wi, pos, k_all, v_all, ki_all):
    bsz, t = q.shape[0], q.shape[1]
    n_top = min(TOPK_ATT, k_all.shape[1] // 4)
    if t > Q_BLOCK and t % Q_BLOCK == 0:
        nb = t // Q_BLOCK

        def to_blocks(a):
            return jnp.moveaxis(a.reshape((bsz, nb, Q_BLOCK) + a.shape[2:]), 1, 0)

        def one_block(args):
            qb, qib, wib, pb = args
            return _dsa_block(qb, qib, wib, pb, k_all, v_all, ki_all, n_top)

        out = lax.map(one_block, (to_blocks(q), to_blocks(qi), to_blocks(wi), pos.reshape(nb, Q_BLOCK)))
        return jnp.moveaxis(out, 0, 1).reshape(bsz, t, N_HEADS * HEAD_DIM)
    return _dsa_block(q, qi, wi, pos, k_all, v_all, ki_all, n_top)


def _mm_body(x_ref, w_ref, o_ref):
    o_ref[...] = jnp.dot(x_ref[...], w_ref[...], preferred_element_type=F32)


def matmul_bf16(x_bf, w_bf, tm, tn):
    m, k = x_bf.shape
    n = w_bf.shape[1]
    return pl.pallas_call(
        _mm_body,
        grid=(m // tm, n // tn),
        in_specs=[pl.BlockSpec((tm, k), lambda i, j: (i, 0)),
                  pl.BlockSpec((k, tn), lambda i, j: (0, j))],
        out_specs=pl.BlockSpec((tm, tn), lambda i, j: (i, j)),
        out_shape=jax.ShapeDtypeStruct((m, n), F32),
        compiler_params=_cparams("arbitrary", "arbitrary"),
    )(x_bf, w_bf)


def _res_ln_body(x_ref, y_ref, g_ref, b_ref, o_ref, ob_ref):
    r = ALPHA * x_ref[...] + y_ref[...]
    mu = jnp.mean(r, axis=-1, keepdims=True)
    rc = r - mu
    var = jnp.mean(rc * rc, axis=-1, keepdims=True)
    out = rc * lax.rsqrt(var + LN_EPS) * g_ref[...] + b_ref[...]
    o_ref[...] = out
    ob_ref[...] = out.astype(BF16)


def residual_layer_norm(x, y, g, b, tb):
    m, d = x.shape
    row = pl.BlockSpec((tb, d), lambda i: (i, 0))
    vec = pl.BlockSpec((1, d), lambda i: (0, 0))
    return pl.pallas_call(
        _res_ln_body,
        grid=(m // tb,),
        in_specs=[row, row, vec, vec],
        out_specs=[row, row],
        out_shape=[jax.ShapeDtypeStruct((m, d), F32), jax.ShapeDtypeStruct((m, d), BF16)],
        compiler_params=_cparams("arbitrary"),
    )(x, y, g.reshape(1, d), b.reshape(1, d))


N_PROMPT = BATCH * SEQ
N_SAMPLE = DEC_BATCH * DEC_SEQ
IN_OFFS = tuple(int(o) for o in np.cumsum(PROJ_SIZES)[:-1])
IN_ALIGNED = IN_OFFS[6]
IN_SMALL = PROJ_SIZES[7] + PROJ_SIZES[8]
IN_TN = 1408


def _in_proj_weight(w_in_l):
    small = jnp.pad(w_in_l[:, IN_ALIGNED:IN_ALIGNED + IN_SMALL], ((0, 0), (0, LANES - IN_SMALL)))
    return jnp.concatenate([w_in_l[:, :IN_ALIGNED], small, w_in_l[:, IN_ALIGNED + IN_SMALL:]], axis=1).astype(BF16)


def _mixers_jax(z, l, bsz, t, pos0, past_k, past_v, past_ki, h0, lru_buf, conf_buf, pool_buf, prm):
    z = z.reshape(bsz, t, -1)
    pos = pos0 + jnp.arange(t, dtype=jnp.int32)
    za, zg, zb = z[..., :1024], z[..., 1024:2048], z[..., 2048:4096]
    zq, zk, zv, zqi = z[..., 4096:5120], z[..., 5120:5376], z[..., 5376:5632], z[..., 5632:6144]
    zki, zwi = z[..., 6144:6208], z[..., 6208:6216]
    zp, zgate = z[..., 6272:7296], z[..., 7296:15488]
    xc, lru_new = _causal_dwconv(lru_buf, za, prm['lru_conv_w'][l], prm['lru_conv_b'][l])
    h, h_last = _rglru(xc, h0, prm['lru_wr'][l], prm['lru_br'][l], prm['lru_wi'][l],
                       prm['lru_bi'][l], prm['lru_lambda'][l])
    y_a = h * jax.nn.gelu(zg)
    glu = zb[..., :W_CONF] * jax.nn.sigmoid(zb[..., W_CONF:])
    cb, conf_new = _causal_dwconv(conf_buf, glu, prm['conf_conv_w'][l], prm['conf_conv_b'][l])
    y_b = jax.nn.silu(_layer_norm(cb, prm['conf_ln_g'][l], prm['conf_ln_b'][l]))
    q = _rope_partial(zq.reshape(bsz, t, N_HEADS, HEAD_DIM), pos, ROT_DIM)
    k = _rope_partial(zk.reshape(bsz, t, N_KV, HEAD_DIM), pos, ROT_DIM)
    v = zv.reshape(bsz, t, N_KV, HEAD_DIM)
    qi = _rope_partial(zqi.reshape(bsz, t, N_IDX_HEADS, IDX_DIM), pos, IDX_ROT)
    ki = _rope_partial(zki[:, :, None, :], pos, IDX_ROT)[:, :, 0, :]
    k_all = jnp.concatenate([past_k, k], axis=1)
    v_all = jnp.concatenate([past_v, v], axis=1)
    ki_all = jnp.concatenate([past_ki, ki], axis=1)
    y_c = _dsa_attention(q, qi, zwi, pos, k_all, v_all, ki_all)
    y_d, pool_new = _pool_mix(pool_buf, zp, prm['pool_w'][l], prm['pool_scale'][l])
    branches = jnp.stack([y_a, y_b, y_c, y_d], axis=2)
    proj = jnp.einsum('btnw,nwd->btnd', branches, prm['w_branch'][l])
    gates = jax.nn.sigmoid(zgate.reshape(bsz, t, N_BRANCH, D_MODEL))
    mix = jnp.sum(gates * proj, axis=2) @ prm['w_out'][l]
    return mix.reshape(bsz * t, D_MODEL), (k, v, ki, h_last, lru_new, conf_new, pool_new)


def kernel(x_prompt, x_sample, cache_k, cache_v, cache_kidx, state_lru_h, state_lru_conv,
           state_conf_conv, state_pool, w_in, lru_conv_w, lru_conv_b, lru_wr, lru_br, lru_wi,
           lru_bi, lru_lambda, conf_conv_w, conf_conv_b, conf_ln_g, conf_ln_b, pool_w, pool_scale,
           w_branch, w_out, ln1_g, ln1_b, peer_wq, peer_subkeys, peer_u, peer_v, ln2_g, ln2_b):
    prm = {
        'lru_conv_w': lru_conv_w, 'lru_conv_b': lru_conv_b, 'lru_wr': lru_wr,
        'lru_br': lru_br, 'lru_wi': lru_wi, 'lru_bi': lru_bi, 'lru_lambda': lru_lambda,
        'conf_conv_w': conf_conv_w, 'conf_conv_b': conf_conv_b, 'conf_ln_g': conf_ln_g,
        'conf_ln_b': conf_ln_b, 'pool_w': pool_w, 'pool_scale': pool_scale, 'w_branch': w_branch,
        'w_out': w_out,
    }
    dt = x_prompt.dtype
    x = jnp.concatenate([x_prompt.reshape(N_PROMPT, D_MODEL), x_sample.reshape(N_SAMPLE, D_MODEL)], axis=0)
    x_bf = x.astype(BF16)
    zeros_p = dict(
        k=jnp.zeros((BATCH, 0, N_KV, HEAD_DIM), dt), ki=jnp.zeros((BATCH, 0, IDX_DIM), dt),
        h=jnp.zeros((BATCH, W_LRU), F32), lru=jnp.zeros((BATCH, LRU_CONV - 1, W_LRU), dt),
        conf=jnp.zeros((BATCH, CONF_CONV - 1, W_CONF), dt), pool=jnp.zeros((BATCH, POOL_BUF, W_POOL), dt))
    st_p, st_s = [], []
    for l in range(DEPTH):
        z = matmul_bf16(x_bf, _in_proj_weight(w_in[l]), 1024, IN_TN)
        mix_p, sp = _mixers_jax(z[:N_PROMPT], l, BATCH, SEQ, 0, zeros_p['k'], zeros_p['k'], zeros_p['ki'],
                                zeros_p['h'], zeros_p['lru'], zeros_p['conf'], zeros_p['pool'], prm)
        mix_s, ss = _mixers_jax(z[N_PROMPT:], l, DEC_BATCH, DEC_SEQ, PAST_LEN, cache_k[l], cache_v[l],
                                cache_kidx[l], state_lru_h[l], state_lru_conv[l], state_conf_conv[l],
                                state_pool[l], prm)
        mix = jnp.concatenate([mix_p, mix_s], axis=0)
        x1, _ = residual_layer_norm(x, mix, ln1_g[l], ln1_b[l], 512)
        ffn = peer_pallas(x1, peer_wq[l], peer_subkeys[l], peer_u[l], peer_v[l])
        x, x_bf = residual_layer_norm(x1, ffn, ln2_g[l], ln2_b[l], 512)
        st_p.append(sp)
        st_s.append(ss)
    outs_p = [jnp.stack([s[i] for s in st_p]) for i in range(7)]
    outs_s = [jnp.stack([s[i] for s in st_s]) for i in range(7)]
    y_prompt = x[:N_PROMPT].reshape(BATCH, SEQ, D_MODEL)
    y_sample = x[N_PROMPT:].reshape(DEC_BATCH, DEC_SEQ, D_MODEL)
    return (y_prompt, y_sample, *outs_p, *outs_s)
```

```python
import functools
import math
import jax, jax.numpy as jnp
from jax import lax
import numpy as np
from jax.experimental import pallas as pl
from jax.experimental.pallas import tpu as pltpu

D_MODEL = 2048
BATCH = 2
SEQ = 4096
DEPTH = 2
DEC_BATCH = 32
DEC_SEQ = 64
PAST_LEN = 2048

CHUNK = 64
W_LRU = 1024
LRU_BLOCKS = 16
LRU_CONV = 4
LRU_C = 8.0
W_CONF = 1024
CONF_CONV = 31
N_HEADS = 8
N_KV = 2
HEAD_DIM = 128
ROT_DIM = HEAD_DIM // 4
N_IDX_HEADS = 8
IDX_DIM = 64
IDX_ROT = IDX_DIM // 4
TOPK_ATT = 256
ROPE_THETA = 500000.0
Q_BLOCK = 128
W_POOL = 1024
POOL_WINDOWS = (2, 4, 8, 16)
POOL_GROUPS = 4
POOL_GW = W_POOL // POOL_GROUPS
POOL_BUF = 15
N_BRANCH = 4
W_BRANCH = 1024
PEER_HEADS = 8
PEER_NKEYS = 128
PEER_TOPK = 16
PEER_EXPERTS = PEER_NKEYS * PEER_NKEYS
PEER_QDIM = 256
PEER_HALF = PEER_QDIM // 2
PEER_BLOCK = 128
ALPHA = (2 * DEPTH) ** 0.25
BETA = (8 * DEPTH) ** -0.25
LN_EPS = 1e-5
PROJ_SIZES = (W_LRU, W_LRU, 2 * W_CONF, N_HEADS * HEAD_DIM, N_KV * HEAD_DIM, N_KV * HEAD_DIM,
              N_IDX_HEADS * IDX_DIM, IDX_DIM, N_IDX_HEADS, W_POOL, N_BRANCH * D_MODEL)
PROJ_TOTAL = 15432

F32 = jnp.float32
BF16 = jnp.bfloat16
VMEM_LIMIT_BYTES = 56 * 1024 * 1024
LANES = 128
GELU_C = math.sqrt(2.0 / math.pi)
I32 = jnp.int32
CHUNK_BITS = 6
INT_MIN = -2 ** 31
NEG_BIG = -1e30


def _cparams(*sem):
    return pltpu.CompilerParams(dimension_semantics=sem, vmem_limit_bytes=VMEM_LIMIT_BYTES)


def _peer_scores_body(x_ref, wq_ref, sk_ref, s_ref, xt_ref, *, n_hc):
    x = x_ref[...]
    xt_ref[...] = x.T.astype(BF16)
    q = jnp.dot(x.astype(BF16), wq_ref[...], preferred_element_type=F32)
    for hc in range(n_hc):
        qh = q[:, hc * LANES:(hc + 1) * LANES].astype(BF16)
        s_ref[hc] = lax.dot_general(sk_ref[hc], qh, (((1,), (1,)), ((), ())), preferred_element_type=F32)


def peer_scores(x, wq_bf, sk_bf, tb):
    m, d = x.shape
    n_hc = sk_bf.shape[0]
    return pl.pallas_call(
        functools.partial(_peer_scores_body, n_hc=n_hc),
        grid=(m // tb,),
        in_specs=[pl.BlockSpec((tb, d), lambda i: (i, 0)),
                  pl.BlockSpec((d, n_hc * LANES), lambda i: (0, 0)),
                  pl.BlockSpec((n_hc, LANES, LANES), lambda i: (0, 0, 0))],
        out_specs=[pl.BlockSpec((n_hc, LANES, tb), lambda i: (0, 0, i)),
                   pl.BlockSpec((d, tb), lambda i: (0, i))],
        out_shape=[jax.ShapeDtypeStruct((n_hc, LANES, m), F32),
                   jax.ShapeDtypeStruct((d, m), BF16)],
        compiler_params=_cparams("arbitrary"),
    )(x, wq_bf, sk_bf)


N_EXTRACT = PEER_TOPK + 1


def _top_desc(s, n):
    rows = lax.broadcasted_iota(jnp.int32, s.shape, 0)
    rem = s
    out = []
    for k in range(n):
        mx = jnp.max(rem, axis=0, keepdims=True)
        out.append(mx)
        if k + 1 < n:
            first = jnp.min(jnp.where(rem == mx, rows, PEER_NKEYS), axis=0, keepdims=True)
            rem = jnp.where(rows == first, -jnp.inf, rem)
    return out


def _stack_rows(vals, nrows, tl):
    rows = lax.broadcasted_iota(jnp.int32, (nrows, tl), 0)
    acc = jnp.full((nrows, tl), -jnp.inf, F32)
    for k, v in enumerate(vals):
        acc = jnp.where(rows == k, v, acc)
    return acc


def _peer_select_body(s_ref, a_ref, b_ref, c_ref, *, n_heads, tl):
    n = N_EXTRACT
    rows8 = lax.broadcasted_iota(jnp.int32, (8, tl), 0)
    for h in range(n_heads):
        s1 = s_ref[2 * h]
        s2 = s_ref[2 * h + 1]
        sv1 = _top_desc(s1, n)
        sv2 = _top_desc(s2, n)
        pieces = [_stack_rows(sv2, 24, tl) + sv1[0]]
        sv2_8 = _stack_rows(sv2[:8], 8, tl)
        for i in range(1, 8):
            nj = n // (i + 1)
            pieces.append(jnp.where(rows8 < nj, sv2_8 + sv1[i], -jnp.inf))
        pieces.append(_stack_rows(sv1[8:], 16, tl) + sv2[0])
        cand = jnp.concatenate(pieces, axis=0)
        taken = jnp.zeros((1, tl), F32)
        v16 = jnp.zeros((1, tl), F32)
        v17 = jnp.zeros((1, tl), F32)
        rem = cand
        for k in range(n):
            mx = jnp.max(rem, axis=0, keepdims=True)
            eq = rem == mx
            cnt = jnp.sum(jnp.where(eq, 1.0, 0.0), axis=0, keepdims=True)
            new_taken = taken + cnt
            v16 = jnp.where((taken < PEER_TOPK) & (new_taken >= PEER_TOPK), mx, v16)
            v17 = jnp.where((taken < n) & (new_taken >= n), mx, v17)
            rem = jnp.where(eq, -jnp.inf, rem)
            taken = new_taken
        thr = 0.5 * (v16 + v17)
        top = sv1[0] + sv2[0]
        z = jnp.sum(jnp.where(cand >= v16, jnp.exp(cand - top), 0.0), axis=0, keepdims=True)
        a_ref[h] = jnp.exp(s1 - sv1[0]) / z
        b_ref[h] = jnp.exp(s2 - sv2[0])
        c_ref[h] = thr - s1


def peer_select(s, tl):
    n_hc, _, m = s.shape
    n_heads = n_hc // 2
    spec = pl.BlockSpec((n_heads, PEER_NKEYS, tl), lambda i: (0, 0, i))
    shp = jax.ShapeDtypeStruct((n_heads, PEER_NKEYS, m), F32)
    return pl.pallas_call(
        functools.partial(_peer_select_body, n_heads=n_heads, tl=tl),
        grid=(m // tl,),
        in_specs=[pl.BlockSpec((n_hc, PEER_NKEYS, tl), lambda i: (0, 0, i))],
        out_specs=[spec, spec, spec],
        out_shape=[shp, shp, shp],
        compiler_params=_cparams("arbitrary"),
    )(s)


def _gelu_tanh(x):
    return 0.5 * x * (1.0 + jnp.tanh(GELU_C * (x + 0.044715 * (x * x * x))))


def _peer_dense_body(xt_ref, u_ref, vt_ref, a_ref, b_ref, c_ref, s2_ref, o_ref, acc_ref, act_ref, coef_ref,
                     *, n_heads, eb, tbt, lc):
    j = pl.program_id(1)

    @pl.when(j == 0)
    def _():
        acc_ref[...] = jnp.zeros_like(acc_ref)

    act_ref[...] = jnp.dot(u_ref[...], xt_ref[...], preferred_element_type=F32)
    nr = eb // PEER_NKEYS

    def lane_chunk(ci, carry):
        l0 = pl.multiple_of(ci * lc, lc)
        for rr in range(nr):
            r = j * nr + rr
            w = jnp.zeros((PEER_NKEYS, lc), F32)
            for h in range(n_heads):
                a = a_ref[h, pl.ds(r, 1), pl.ds(l0, lc)]
                c = c_ref[h, pl.ds(r, 1), pl.ds(l0, lc)]
                s2 = s2_ref[h, :, pl.ds(l0, lc)]
                b = b_ref[h, :, pl.ds(l0, lc)]
                w = w + jnp.where(s2 >= c, a * b, 0.0)
            x = act_ref[rr * PEER_NKEYS:(rr + 1) * PEER_NKEYS, pl.ds(l0, lc)]
            coef_ref[rr * PEER_NKEYS:(rr + 1) * PEER_NKEYS, pl.ds(l0, lc)] = (w * _gelu_tanh(x)).astype(BF16)
        return carry

    lax.fori_loop(0, tbt // lc, lane_chunk, 0)
    acc_ref[...] += jnp.dot(vt_ref[...], coef_ref[...], preferred_element_type=F32)

    @pl.when(j == pl.num_programs(1) - 1)
    def _():
        o_ref[...] = acc_ref[...].T


def peer_dense(xt_bf, u_bf, vt_bf, a, b, c, s, tbt, eb, lc=256):
    d, m = xt_bf.shape
    e = u_bf.shape[0]
    n_heads = a.shape[0]
    s4 = s.reshape(n_heads, 2, PEER_NKEYS, m)
    hspec = pl.BlockSpec((n_heads, PEER_NKEYS, tbt), lambda i, j: (0, 0, i))
    return pl.pallas_call(
        functools.partial(_peer_dense_body, n_heads=n_heads, eb=eb, tbt=tbt, lc=lc),
        grid=(m // tbt, e // eb),
        in_specs=[pl.BlockSpec((d, tbt), lambda i, j: (0, i)),
                  pl.BlockSpec((eb, d), lambda i, j: (j, 0)),
                  pl.BlockSpec((d, eb), lambda i, j: (0, j)),
                  hspec, hspec, hspec,
                  pl.BlockSpec((n_heads, None, PEER_NKEYS, tbt), lambda i, j: (0, 1, 0, i))],
        out_specs=pl.BlockSpec((tbt, d), lambda i, j: (i, 0)),
        out_shape=jax.ShapeDtypeStruct((m, d), F32),
        scratch_shapes=[pltpu.VMEM((d, tbt), F32), pltpu.VMEM((eb, tbt), F32), pltpu.VMEM((eb, tbt), BF16)],
        compiler_params=_cparams("arbitrary", "arbitrary"),
    )(xt_bf, u_bf, vt_bf, a, b, c, s4)


def peer_pallas(x, wq, sub_keys, u_tab, v_tab, tb=512, tl=256, tbt=512, eb=512):
    n_heads = sub_keys.shape[0]
    sk_bf = sub_keys.reshape(n_heads * 2, PEER_NKEYS, -1).astype(BF16)
    s, xt = peer_scores(x, wq.astype(BF16), sk_bf, tb)
    a, b, c = peer_select(s, tl)
    return peer_dense(xt, u_tab.astype(BF16), v_tab.T.astype(BF16), a, b, c, s, tbt, eb)


def rope_tables(pos):
    posf = pos.astype(F32)[:, None]

    def table(rot, period, lanes_active):
        half = rot // 2
        inv = ROPE_THETA ** (-jnp.arange(half, dtype=F32) / half)
        ang = posf * inv[None, :]
        cos, sin = jnp.cos(ang), jnp.sin(ang)
        m = pos.shape[0]
        c = jnp.concatenate([cos, cos, jnp.ones((m, period - rot), F32)], axis=1)
        s = jnp.concatenate([-sin, sin, jnp.zeros((m, period - rot), F32)], axis=1)
        reps = LANES // period
        c, s = jnp.tile(c, (1, reps)), jnp.tile(s, (1, reps))
        lane = jnp.arange(LANES)[None, :]
        c = jnp.where(lane < lanes_active, c, 1.0)
        s = jnp.where(lane < lanes_active, s, 0.0)
        return c, s

    ch, sh = table(ROT_DIM, HEAD_DIM, LANES)
    ci, si = table(IDX_ROT, IDX_DIM, LANES)
    ck, sk = table(IDX_ROT, IDX_DIM, IDX_DIM)
    return ch, sh, ci, si, ck, sk


def _swap_halves(x, half, period):
    lane = lax.broadcasted_iota(I32, x.shape, 1)
    return jnp.where((lane % period) < half, pltpu.roll(x, LANES - half, 1), pltpu.roll(x, half, 1))


def _dsa_prep_body(q_ref, k_ref, v_ref, qi_ref, sm_ref, ch_ref, sh_ref, ci_ref, si_ref, ck_ref, sk_ref,
                   qo_ref, ko_ref, kb_ref, vb_ref, qio_ref, smo_ref, kib_ref):
    ch, sh = ch_ref[...], sh_ref[...]
    for h in range(N_HEADS):
        x = q_ref[:, h * LANES:(h + 1) * LANES]
        qo_ref[:, h * LANES:(h + 1) * LANES] = (x * ch + _swap_halves(x, ROT_DIM // 2, HEAD_DIM) * sh).astype(BF16)
    for h in range(N_KV):
        x = k_ref[:, h * LANES:(h + 1) * LANES]
        kr = x * ch + _swap_halves(x, ROT_DIM // 2, HEAD_DIM) * sh
        ko_ref[:, h * LANES:(h + 1) * LANES] = kr
        kb_ref[:, h * LANES:(h + 1) * LANES] = kr.astype(BF16)
    vb_ref[...] = v_ref[...].astype(BF16)
    ci, si = ci_ref[...], si_ref[...]
    for p in range(N_IDX_HEADS * IDX_DIM // LANES):
        x = qi_ref[:, p * LANES:(p + 1) * LANES]
        qio_ref[:, p * LANES:(p + 1) * LANES] = (x * ci + _swap_halves(x, IDX_ROT // 2, IDX_DIM) * si).astype(BF16)
    x = sm_ref[...]
    sm = x * ck_ref[...] + _swap_halves(x, IDX_ROT // 2, IDX_DIM) * sk_ref[...]
    smo_ref[...] = sm
    kib_ref[...] = sm.astype(BF16)


def dsa_prep(z, tables, tb, col):
    m = z.shape[0]

    def zspec(width, start):
        return pl.BlockSpec((tb, width), lambda i: (i, start // width))

    def ospec(width):
        return pl.BlockSpec((tb, width), lambda i: (i, 0))

    tspec = pl.BlockSpec((tb, LANES), lambda i: (i, 0))
    nq, nkv, nqi = N_HEADS * HEAD_DIM, N_KV * HEAD_DIM, N_IDX_HEADS * IDX_DIM
    return pl.pallas_call(
        _dsa_prep_body,
        grid=(m // tb,),
        in_specs=[zspec(nq, col['q']), zspec(nkv, col['k']), zspec(nkv, col['v']), zspec(nqi, col['qi']),
                  zspec(LANES, col['small'])] + [tspec] * 6,
        out_specs=[ospec(nq), ospec(nkv), ospec(nkv), ospec(nkv), ospec(nqi), ospec(LANES), ospec(LANES)],
        out_shape=[jax.ShapeDtypeStruct((m, nq), BF16), jax.ShapeDtypeStruct((m, nkv), F32),
                   jax.ShapeDtypeStruct((m, nkv), BF16), jax.ShapeDtypeStruct((m, nkv), BF16),
                   jax.ShapeDtypeStruct((m, nqi), BF16), jax.ShapeDtypeStruct((m, LANES), F32),
                   jax.ShapeDtypeStruct((m, LANES), BF16)],
        compiler_params=_cparams("arbitrary"),
    )(z, z, z, z, z, *tables)


def _sortable(x):
    b = lax.bitcast_convert_type(x + 0.0, I32)
    return b ^ ((b >> 31) & 0x7FFFFFFF)


def _index_queries(qi, sm):
    lane = lax.broadcasted_iota(I32, (qi.shape[0], LANES), 1)
    qs, ws = [], []
    for h in range(N_IDX_HEADS):
        x = qi[:, (h // 2) * LANES:(h // 2 + 1) * LANES].astype(F32)
        if h % 2 == 1:
            x = pltpu.roll(x, IDX_DIM, 1)
        qs.append(jnp.where(lane < IDX_DIM, x, 0.0).astype(BF16))
        ws.append(sm[:, IDX_DIM + h:IDX_DIM + h + 1])
    return qs, ws


def _index_scores(qs, ws, ki_tile):
    score = None
    for q, w in zip(qs, ws):
        d = lax.dot_general(q, ki_tile, (((1,), (1,)), ((), ())), preferred_element_type=F32)
        t = w * jnp.maximum(d, 0.0)
        score = t if score is None else score + t
    return score


def _count(keys_ref, n_tiles, kt, rows, pred):
    def body(t, acc):
        base = pl.multiple_of(t * kt, kt)
        for c in range(kt // LANES):
            tile = keys_ref[:, pl.ds(base + c * LANES, LANES)]
            idx = base + c * LANES + lax.broadcasted_iota(I32, (rows, LANES), 1)
            acc = acc + jnp.where(pred(tile, idx), 1.0, 0.0)
        return acc

    acc = lax.fori_loop(0, n_tiles, body, jnp.zeros((rows, LANES), F32))
    return jnp.sum(acc, axis=1, keepdims=True)


def _select_threshold(keys_ref, n_tiles, kt, rows, n_sel, n_lane_bits, j_ref):
    def bcast(v):
        return jnp.broadcast_to(v, (rows, LANES))

    ge0 = _count(keys_ref, n_tiles, kt, rows, lambda k, i: k >= 0)
    lo0 = jnp.where(ge0 >= n_sel, 0, INT_MIN).astype(I32)

    def bit_step(i, lo):
        cand = lo + jnp.left_shift(jnp.int32(1), 30 - i)
        cb = bcast(cand)
        cnt = _count(keys_ref, n_tiles, kt, rows, lambda k, idx: k >= cb)
        return jnp.where(cnt >= n_sel, cand, lo)

    tau = lax.fori_loop(0, 31, bit_step, lo0)
    tb = bcast(tau)
    n_ge = _count(keys_ref, n_tiles, kt, rows, lambda k, i: k >= tb)
    n_lanes_max = 1 << n_lane_bits
    j_ref[...] = jnp.full((rows, 1), n_lanes_max, I32)

    @pl.when(jnp.max(n_ge - n_sel) > 0.0)
    def _():
        n_gt = _count(keys_ref, n_tiles, kt, rows, lambda k, i: k > tb)
        need = n_sel - n_gt

        def idx_step(i, j):
            cand = j + jnp.left_shift(jnp.int32(1), n_lane_bits - i)
            cb = bcast(cand)
            cnt = _count(keys_ref, n_tiles, kt, rows, lambda k, idx: (k == tb) & (idx < cb))
            return jnp.where((cand <= n_lanes_max) & (cnt <= need), cand, j)

        j_ref[...] = lax.fori_loop(0, n_lane_bits + 1, idx_step, jnp.zeros((rows, 1), I32))

    return tau


def _attend_tile(qg, k_tile, v_tile, sel, m_ref, l_ref, acc_ref):
    per = N_HEADS // N_KV
    tq, w = sel.shape
    scale = HEAD_DIM ** -0.5
    for g in range(N_KV):
        s = lax.dot_general(qg[g], k_tile[:, g * HEAD_DIM:(g + 1) * HEAD_DIM], (((1,), (1,)), ((), ())),
                            preferred_element_type=F32) * scale
        s = jnp.where(sel[None], s.reshape(per, tq, w), NEG_BIG)
        m_old = m_ref[g]
        m_new = jnp.maximum(m_old, jnp.max(s, axis=2, keepdims=True))
        alpha = jnp.exp(m_old - m_new)
        p = jnp.where(sel[None], jnp.exp(s - m_new), 0.0)
        l_ref[g] = alpha * l_ref[g] + jnp.sum(p, axis=2, keepdims=True)
        pv = jnp.dot(p.reshape(per * tq, w).astype(BF16), v_tile[:, g * HEAD_DIM:(g + 1) * HEAD_DIM],
                     preferred_element_type=F32)
        acc_ref[g] = alpha * acc_ref[g] + pv.reshape(per, tq, HEAD_DIM)
        m_ref[g] = m_new


def _query_stacks(q):
    per = N_HEADS // N_KV
    return [jnp.concatenate([q[:, (g * per + i) * HEAD_DIM:(g * per + i + 1) * HEAD_DIM] for i in range(per)], axis=0)
            for g in range(N_KV)]


def _init_softmax(m_ref, l_ref, acc_ref):
    m_ref[...] = jnp.full(m_ref.shape, NEG_BIG, F32)
    l_ref[...] = jnp.zeros(l_ref.shape, F32)
    acc_ref[...] = jnp.zeros(acc_ref.shape, F32)


def _write_heads(o_ref, l_ref, acc_ref, tq):
    per = N_HEADS // N_KV
    for g in range(N_KV):
        o = acc_ref[g] / l_ref[g]
        for i in range(per):
            h = g * per + i
            o_ref[:, h * HEAD_DIM:(h + 1) * HEAD_DIM] = o[i].astype(o_ref.dtype)


def _selected(keys, idx, tau, j):
    return (keys > tau) | ((keys == tau) & (idx < j))


def _dsa_prompt_body(q_ref, qi_ref, sm_ref, k_ref, v_ref, ki_ref, o_ref,
                     keys_ref, j_ref, m_ref, l_ref, acc_ref, *, tq, kt, n_lane_bits):
    jq = pl.program_id(1)
    p0 = jq * tq
    row = lax.broadcasted_iota(I32, (tq, 1), 0)
    limit = p0 + (jnp.right_shift(row, CHUNK_BITS) + 1) * CHUNK
    n_tiles = (p0 + tq + kt - 1) // kt
    qs, ws = _index_queries(qi_ref[...], sm_ref[...])

    def score_tile(t, carry):
        base = pl.multiple_of(t * kt, kt)
        sc = _index_scores(qs, ws, ki_ref[pl.ds(base, kt), :])
        idx = base + lax.broadcasted_iota(I32, (tq, kt), 1)
        keys_ref[:, pl.ds(base, kt)] = jnp.where(idx < limit, _sortable(sc), INT_MIN)
        return carry

    lax.fori_loop(0, n_tiles, score_tile, 0)
    n_sel = jnp.minimum(limit, TOPK_ATT).astype(F32)
    tau = _select_threshold(keys_ref, n_tiles, kt, tq, n_sel, n_lane_bits, j_ref)
    jb = j_ref[...]
    qg = _query_stacks(q_ref[...])
    _init_softmax(m_ref, l_ref, acc_ref)

    def attend(t, carry):
        base = pl.multiple_of(t * kt, kt)
        idx = base + lax.broadcasted_iota(I32, (tq, kt), 1)
        sel = _selected(keys_ref[:, pl.ds(base, kt)], idx, tau, jb)
        _attend_tile(qg, k_ref[pl.ds(base, kt), :], v_ref[pl.ds(base, kt), :], sel, m_ref, l_ref, acc_ref)
        return carry

    lax.fori_loop(0, n_tiles, attend, 0)
    _write_heads(o_ref, l_ref, acc_ref, tq)


def dsa_prompt(q_bf, qi_bf, sm, k_bf, v_bf, ki_bf, bsz, t, tq, kt):
    nq = t // tq
    n_lane_bits = int(math.log2(t))
    nqd, nkv, nqi = N_HEADS * HEAD_DIM, N_KV * HEAD_DIM, N_IDX_HEADS * IDX_DIM
    per = N_HEADS // N_KV

    def qspec(w):
        return pl.BlockSpec((tq, w), lambda b, j: (b * nq + j, 0))

    def kspec(w):
        return pl.BlockSpec((t, w), lambda b, j: (b, 0))

    return pl.pallas_call(
        functools.partial(_dsa_prompt_body, tq=tq, kt=kt, n_lane_bits=n_lane_bits),
        grid=(bsz, nq),
        in_specs=[qspec(nqd), qspec(nqi), qspec(LANES), kspec(nkv), kspec(nkv), kspec(LANES)],
        out_specs=qspec(nqd),
        out_shape=jax.ShapeDtypeStruct((bsz * t, nqd), BF16),
        scratch_shapes=[pltpu.VMEM((tq, t), I32), pltpu.VMEM((tq, 1), I32),
                        pltpu.VMEM((N_KV, per, tq, 1), F32), pltpu.VMEM((N_KV, per, tq, 1), F32),
                        pltpu.VMEM((N_KV, per, tq, HEAD_DIM), F32)],
        compiler_params=_cparams("arbitrary", "arbitrary"),
    )(q_bf, qi_bf, sm, k_bf, v_bf, ki_bf)


def _dsa_sample_body(q_ref, qi_ref, sm_ref, kn_ref, vn_ref, kin_ref, ck_ref, cv_ref, cki_ref, o_ref,
                     keys_ref, j_ref, m_ref, l_ref, acc_ref, *, tq, kt, past, n_lane_bits):
    n_cache_tiles = past // kt
    n_new = LANES
    n_keys = past + tq
    lanes_total = past + n_new
    qs, ws = _index_queries(qi_ref[...], sm_ref[...])

    def pad_rows(x):
        return jnp.concatenate([x, jnp.zeros((n_new - tq, x.shape[1]), x.dtype)], axis=0)

    for t in range(n_cache_tiles):
        sc = _index_scores(qs, ws, cki_ref[t * kt:(t + 1) * kt, :])
        keys_ref[:, t * kt:(t + 1) * kt] = _sortable(sc)
    sc = _index_scores(qs, ws, pad_rows(kin_ref[...]))
    idx_new = past + lax.broadcasted_iota(I32, (tq, n_new), 1)
    keys_ref[:, past:lanes_total] = jnp.where(idx_new < n_keys, _sortable(sc), INT_MIN)

    n_sel = jnp.full((tq, 1), float(min(TOPK_ATT, n_keys // 4)), F32)
    tau = _select_threshold(keys_ref, lanes_total // LANES, LANES, tq, n_sel, n_lane_bits, j_ref)
    jb = j_ref[...]
    qg = _query_stacks(q_ref[...])
    _init_softmax(m_ref, l_ref, acc_ref)
    for t in range(n_cache_tiles):
        idx = t * kt + lax.broadcasted_iota(I32, (tq, kt), 1)
        sel = _selected(keys_ref[:, t * kt:(t + 1) * kt], idx, tau, jb)
        _attend_tile(qg, ck_ref[t * kt:(t + 1) * kt, :].astype(BF16), cv_ref[t * kt:(t + 1) * kt, :].astype(BF16),
                     sel, m_ref, l_ref, acc_ref)
    sel = _selected(keys_ref[:, past:lanes_total], idx_new, tau, jb)
    _attend_tile(qg, pad_rows(kn_ref[...]), pad_rows(vn_ref[...]), sel, m_ref, l_ref, acc_ref)
    _write_heads(o_ref, l_ref, acc_ref, tq)


def dsa_sample(q_bf, qi_bf, sm, k_bf, v_bf, ki_bf, cache_k, cache_v, cache_ki_bf, row0, bsz, tq, kt):
    past = cache_k.shape[1]
    n_lane_bits = int(math.ceil(math.log2(past + LANES)))
    nqd, nkv, nqi = N_HEADS * HEAD_DIM, N_KV * HEAD_DIM, N_IDX_HEADS * IDX_DIM
    per = N_HEADS // N_KV
    b0 = row0 // tq

    def qspec(w):
        return pl.BlockSpec((tq, w), lambda b: (b0 + b, 0))

    def cspec(w):
        return pl.BlockSpec((None, past, w), lambda b: (b, 0, 0))

    return pl.pallas_call(
        functools.partial(_dsa_sample_body, tq=tq, kt=kt, past=past, n_lane_bits=n_lane_bits),
        grid=(bsz,),
        in_specs=[qspec(nqd), qspec(nqi), qspec(LANES), qspec(nkv), qspec(nkv), qspec(LANES),
                  cspec(nkv), cspec(nkv), cspec(LANES)],
        out_specs=pl.BlockSpec((tq, nqd), lambda b: (b, 0)),
        out_shape=jax.ShapeDtypeStruct((bsz * tq, nqd), BF16),
        scratch_shapes=[pltpu.VMEM((tq, past + LANES), I32), pltpu.VMEM((tq, 1), I32),
                        pltpu.VMEM((N_KV, per, tq, 1), F32), pltpu.VMEM((N_KV, per, tq, 1), F32),
                        pltpu.VMEM((N_KV, per, tq, HEAD_DIM), F32)],
        compiler_params=_cparams("arbitrary"),
    )(q_bf, qi_bf, sm, k_bf, v_bf, ki_bf, cache_k, cache_v, cache_ki_bf)


def _layer_norm(x, g, b):
    xf = x.astype(jnp.float32)
    mu = jnp.mean(xf, -1, keepdims=True)
    var = jnp.mean(jnp.square(xf - mu), -1, keepdims=True)
    y = (xf - mu) * lax.rsqrt(var + LN_EPS) * g.astype(jnp.float32) + b.astype(jnp.float32)
    return y.astype(x.dtype)


def _rope_partial(x, pos, rot):
    half = rot // 2
    inv = ROPE_THETA ** (-jnp.arange(half, dtype=jnp.float32) / half)
    ang = pos.astype(jnp.float32)[:, None] * inv[None, :]
    cos = jnp.cos(ang)[None, :, None, :]
    sin = jnp.sin(ang)[None, :, None, :]
    xr = x[..., :rot].astype(jnp.float32)
    x1, x2 = xr[..., :half], xr[..., half:]
    rot_part = jnp.concatenate([x1 * cos - x2 * sin, x2 * cos + x1 * sin], -1)
    return jnp.concatenate([rot_part.astype(x.dtype), x[..., rot:]], -1)


def _causal_dwconv(buf, x, w, b):
    t = x.shape[1]
    xp = jnp.concatenate([buf.astype(x.dtype), x], axis=1)
    y = lax.conv_general_dilated(xp, w.astype(x.dtype)[:, None, :], (1,), 'VALID',
                                 dimension_numbers=('NWC', 'WIO', 'NWC'),
                                 feature_group_count=x.shape[-1])
    return y + b.astype(x.dtype), xp[:, t:]


def _lin_combine(left, right):
    a_l, b_l = left
    a_r, b_r = right
    return a_l * a_r, a_r * b_l + b_r


def _rglru(xc, h0, w_r, b_r, w_i, b_i, lam):
    bsz, t, c = xc.shape
    xb = xc.reshape(bsz, t, LRU_BLOCKS, c // LRU_BLOCKS)
    gr = jnp.einsum('btnc,ncd->btnd', xb, w_r).reshape(bsz, t, c) + b_r
    gi = jnp.einsum('btnc,ncd->btnd', xb, w_i).reshape(bsz, t, c) + b_i
    r = jax.nn.sigmoid(gr.astype(jnp.float32))
    i = jax.nn.sigmoid(gi.astype(jnp.float32))
    log_a = -LRU_C * r * jax.nn.softplus(-lam.astype(jnp.float32))
    a = jnp.exp(log_a)
    u = jnp.sqrt(-jnp.expm1(2.0 * log_a)) * i * xc.astype(jnp.float32)
    u = u.at[:, 0].add(a[:, 0] * h0.astype(jnp.float32))
    _, h = lax.associative_scan(_lin_combine, (a, u), axis=1)
    return h, h[:, -1]


def _pool_mix(buf, xin, w_pool, scale):
    bsz, t, c = xin.shape
    xp = jnp.concatenate([buf.astype(xin.dtype), xin], axis=1)
    xf = xp.astype(jnp.float32)
    cs = jnp.concatenate([jnp.zeros((bsz, 1, c), jnp.float32), jnp.cumsum(xf, axis=1)], axis=1)
    means = []
    for g, w in enumerate(POOL_WINDOWS):
        lo = g * POOL_GW
        hi = lo + POOL_GW
        win_sum = cs[:, POOL_BUF + 1:, lo:hi] - cs[:, POOL_BUF + 1 - w:POOL_BUF + 1 - w + t, lo:hi]
        means.append(win_sum * (1.0 / w))
    diff = jnp.concatenate(means, -1) - xf[:, POOL_BUF:]
    diff = diff.astype(xin.dtype).reshape(bsz, t, POOL_GROUPS, POOL_GW)
    y = jnp.einsum('btgc,gcd->btgd', diff, w_pool).reshape(bsz, t, c) * scale
    return y, xp[:, t:]


def _dsa_block(q, qi, wi, q_pos, k_all, v_all, ki_all, n_top):
    bsz, tq = q.shape[0], q.shape[1]
    n_keys = k_all.shape[1]
    limit = (q_pos // CHUNK + 1) * CHUNK
    adm = jnp.arange(n_keys, dtype=jnp.int32)[None, :] < limit[:, None]
    dots = jnp.einsum('bqhd,bsd->bqhs', qi, ki_all).astype(jnp.float32)
    score = jnp.einsum('bqh,bqhs->bqs', wi.astype(jnp.float32), jax.nn.relu(dots))
    score = jnp.where(adm[None], score, -jnp.inf)
    _, idx = lax.top_k(score, n_top)
    valid = idx < limit[None, :, None]
    gather = jax.vmap(lambda rows, ids: rows[ids])
    kg = gather(k_all, idx)
    vg = gather(v_all, idx)
    qg = q.reshape(bsz, tq, N_KV, N_HEADS // N_KV, HEAD_DIM)
    s = jnp.einsum('bqgrd,bqkgd->bqgrk', qg, kg).astype(jnp.float32) * (HEAD_DIM ** -0.5)
    s = jnp.where(valid[:, :, None, None, :], s, -jnp.inf)
    p = jax.nn.softmax(s, axis=-1).astype(vg.dtype)
    o = jnp.einsum('bqgrk,bqkgd->bqgrd', p, vg)
    return o.reshape(bsz, tq, N_HEADS * HEAD_DIM)


def _dsa_attention(q, qi, wi, pos, k_all, v_all, ki_all):
    bsz, t = q.shape[0], q.shape[1]
    n_top = min(TOPK_ATT, k_all.shape[1] // 4)
    if t > Q_BLOCK and t % Q_BLOCK == 0:
        nb = t // Q_BLOCK

        def to_blocks(a):
            return jnp.moveaxis(a.reshape((bsz, nb, Q_BLOCK) + a.shape[2:]), 1, 0)

        def one_block(args):
            qb, qib, wib, pb = args
            return _dsa_block(qb, qib, wib, pb, k_all, v_all, ki_all, n_top)

        out = lax.map(one_block, (to_blocks(q), to_blocks(qi), to_blocks(wi), pos.reshape(nb, Q_BLOCK)))
        return jnp.moveaxis(out, 0, 1).reshape(bsz, t, N_HEADS * HEAD_DIM)
    return _dsa_block(q, qi, wi, pos, k_all, v_all, ki_all, n_top)


def _mm_body(x_ref, w_ref, o_ref):
    o_ref[...] = jnp.dot(x_ref[...], w_ref[...], preferred_element_type=F32)


def matmul_bf16(x_bf, w_bf, tm, tn):
    m, k = x_bf.shape
    n = w_bf.shape[1]
    return pl.pallas_call(
        _mm_body,
        grid=(m // tm, n // tn),
        in_specs=[pl.BlockSpec((tm, k), lambda i, j: (i, 0)),
                  pl.BlockSpec((k, tn), lambda i, j: (0, j))],
        out_specs=pl.BlockSpec((tm, tn), lambda i, j: (i, j)),
        out_shape=jax.ShapeDtypeStruct((m, n), F32),
        compiler_params=_cparams("arbitrary", "arbitrary"),
    )(x_bf, w_bf)


def _res_ln_body(x_ref, y_ref, g_ref, b_ref, o_ref, ob_ref):
    r = ALPHA * x_ref[...] + y_ref[...]
    mu = jnp.mean(r, axis=-1, keepdims=True)
    rc = r - mu
    var = jnp.mean(rc * rc, axis=-1, keepdims=True)
    out = rc * lax.rsqrt(var + LN_EPS) * g_ref[...] + b_ref[...]
    o_ref[...] = out
    ob_ref[...] = out.astype(BF16)


def residual_layer_norm(x, y, g, b, tb):
    m, d = x.shape
    row = pl.BlockSpec((tb, d), lambda i: (i, 0))
    vec = pl.BlockSpec((1, d), lambda i: (0, 0))
    return pl.pallas_call(
        _res_ln_body,
        grid=(m // tb,),
        in_specs=[row, row, vec, vec],
        out_specs=[row, row],
        out_shape=[jax.ShapeDtypeStruct((m, d), F32), jax.ShapeDtypeStruct((m, d), BF16)],
        compiler_params=_cparams("arbitrary"),
    )(x, y, g.reshape(1, d), b.reshape(1, d))


N_PROMPT = BATCH * SEQ
N_SAMPLE = DEC_BATCH * DEC_SEQ
IN_OFFS = tuple(int(o) for o in np.cumsum(PROJ_SIZES)[:-1])
IN_ALIGNED = IN_OFFS[6]
IN_SMALL = PROJ_SIZES[7] + PROJ_SIZES[8]
IN_TN = 1408
DSA_COLS = dict(q=IN_OFFS[2], k=IN_OFFS[3], v=IN_OFFS[4], qi=IN_OFFS[5], small=IN_ALIGNED)
DSA_PREP_TB = 512
DSA_KEY_TILE = 512


def _in_proj_weight(w_in_l):
    small = jnp.pad(w_in_l[:, IN_ALIGNED:IN_ALIGNED + IN_SMALL], ((0, 0), (0, LANES - IN_SMALL)))
    return jnp.concatenate([w_in_l[:, :IN_ALIGNED], small, w_in_l[:, IN_ALIGNED + IN_SMALL:]], axis=1).astype(BF16)


def _mixers_jax(z, y_c, l, bsz, t, h0, lru_buf, conf_buf, pool_buf, prm):
    z = z.reshape(bsz, t, -1)
    y_c = y_c.astype(F32).reshape(bsz, t, -1)
    za, zg, zb = z[..., :1024], z[..., 1024:2048], z[..., 2048:4096]
    zp, zgate = z[..., 6272:7296], z[..., 7296:15488]
    xc, lru_new = _causal_dwconv(lru_buf, za, prm['lru_conv_w'][l], prm['lru_conv_b'][l])
    h, h_last = _rglru(xc, h0, prm['lru_wr'][l], prm['lru_br'][l], prm['lru_wi'][l],
                       prm['lru_bi'][l], prm['lru_lambda'][l])
    y_a = h * jax.nn.gelu(zg)
    glu = zb[..., :W_CONF] * jax.nn.sigmoid(zb[..., W_CONF:])
    cb, conf_new = _causal_dwconv(conf_buf, glu, prm['conf_conv_w'][l], prm['conf_conv_b'][l])
    y_b = jax.nn.silu(_layer_norm(cb, prm['conf_ln_g'][l], prm['conf_ln_b'][l]))
    y_d, pool_new = _pool_mix(pool_buf, zp, prm['pool_w'][l], prm['pool_scale'][l])
    branches = jnp.stack([y_a, y_b, y_c, y_d], axis=2)
    proj = jnp.einsum('btnw,nwd->btnd', branches, prm['w_branch'][l])
    gates = jax.nn.sigmoid(zgate.reshape(bsz, t, N_BRANCH, D_MODEL))
    mix = jnp.sum(gates * proj, axis=2) @ prm['w_out'][l]
    return mix.reshape(bsz * t, D_MODEL), (h_last, lru_new, conf_new, pool_new)


def kernel(x_prompt, x_sample, cache_k, cache_v, cache_kidx, state_lru_h, state_lru_conv,
           state_conf_conv, state_pool, w_in, lru_conv_w, lru_conv_b, lru_wr, lru_br, lru_wi,
           lru_bi, lru_lambda, conf_conv_w, conf_conv_b, conf_ln_g, conf_ln_b, pool_w, pool_scale,
           w_branch, w_out, ln1_g, ln1_b, peer_wq, peer_subkeys, peer_u, peer_v, ln2_g, ln2_b):
    prm = {
        'lru_conv_w': lru_conv_w, 'lru_conv_b': lru_conv_b, 'lru_wr': lru_wr,
        'lru_br': lru_br, 'lru_wi': lru_wi, 'lru_bi': lru_bi, 'lru_lambda': lru_lambda,
        'conf_conv_w': conf_conv_w, 'conf_conv_b': conf_conv_b, 'conf_ln_g': conf_ln_g,
        'conf_ln_b': conf_ln_b, 'pool_w': pool_w, 'pool_scale': pool_scale, 'w_branch': w_branch,
        'w_out': w_out,
    }
    dt = x_prompt.dtype
    x = jnp.concatenate([x_prompt.reshape(N_PROMPT, D_MODEL), x_sample.reshape(N_SAMPLE, D_MODEL)], axis=0)
    x_bf = x.astype(BF16)
    zeros_p = dict(
        h=jnp.zeros((BATCH, W_LRU), F32), lru=jnp.zeros((BATCH, LRU_CONV - 1, W_LRU), dt),
        conf=jnp.zeros((BATCH, CONF_CONV - 1, W_CONF), dt), pool=jnp.zeros((BATCH, POOL_BUF, W_POOL), dt))
    pos = jnp.concatenate([jnp.tile(jnp.arange(SEQ, dtype=I32), BATCH),
                           jnp.tile(PAST_LEN + jnp.arange(DEC_SEQ, dtype=I32), DEC_BATCH)])
    tables = rope_tables(pos)
    nkv = N_KV * HEAD_DIM
    st_p, st_s = [], []
    for l in range(DEPTH):
        z = matmul_bf16(x_bf, _in_proj_weight(w_in[l]), 1024, IN_TN)
        q_bf, k_rot, k_bf, v_bf, qi_bf, sm, ki_bf = dsa_prep(z, tables, DSA_PREP_TB, DSA_COLS)
        yc_p = dsa_prompt(q_bf, qi_bf, sm, k_bf, v_bf, ki_bf, BATCH, SEQ, Q_BLOCK, DSA_KEY_TILE)
        cache_ki_bf = jnp.pad(cache_kidx[l], ((0, 0), (0, 0), (0, LANES - IDX_DIM))).astype(BF16)
        yc_s = dsa_sample(q_bf, qi_bf, sm, k_bf, v_bf, ki_bf, cache_k[l].reshape(DEC_BATCH, PAST_LEN, nkv),
                          cache_v[l].reshape(DEC_BATCH, PAST_LEN, nkv), cache_ki_bf, N_PROMPT, DEC_BATCH, DEC_SEQ,
                          DSA_KEY_TILE)
        v_new = z[:, DSA_COLS['v']:DSA_COLS['v'] + nkv]
        ki_new = sm[:, :IDX_DIM]
        mix_p, sp = _mixers_jax(z[:N_PROMPT], yc_p, l, BATCH, SEQ,
                                zeros_p['h'], zeros_p['lru'], zeros_p['conf'], zeros_p['pool'], prm)
        mix_s, ss = _mixers_jax(z[N_PROMPT:], yc_s, l, DEC_BATCH, DEC_SEQ, state_lru_h[l], state_lru_conv[l],
                                state_conf_conv[l], state_pool[l], prm)
        sp = (k_rot[:N_PROMPT].reshape(BATCH, SEQ, N_KV, HEAD_DIM), v_new[:N_PROMPT].reshape(BATCH, SEQ, N_KV, HEAD_DIM),
              ki_new[:N_PROMPT].reshape(BATCH, SEQ, IDX_DIM)) + sp
        ss = (k_rot[N_PROMPT:].reshape(DEC_BATCH, DEC_SEQ, N_KV, HEAD_DIM),
              v_new[N_PROMPT:].reshape(DEC_BATCH, DEC_SEQ, N_KV, HEAD_DIM),
              ki_new[N_PROMPT:].reshape(DEC_BATCH, DEC_SEQ, IDX_DIM)) + ss
        mix = jnp.concatenate([mix_p, mix_s], axis=0)
        x1, _ = residual_layer_norm(x, mix, ln1_g[l], ln1_b[l], 512)
        ffn = peer_pallas(x1, peer_wq[l], peer_subkeys[l], peer_u[l], peer_v[l])
        x, x_bf = residual_layer_norm(x1, ffn, ln2_g[l], ln2_b[l], 512)
        st_p.append(sp)
        st_s.append(ss)
    outs_p = [jnp.stack([s[i] for s in st_p]) for i in range(7)]
    outs_s = [jnp.stack([s[i] for s in st_s]) for i in range(7)]
    y_prompt = x[:N_PROMPT].reshape(BATCH, SEQ, D_MODEL)
    y_sample = x[N_PROMPT:].reshape(DEC_BATCH, DEC_SEQ, D_MODEL)
    return (y_prompt, y_sample, *outs_p, *outs_s)
```

```python
import functools
import math
import jax, jax.numpy as jnp
from jax import lax
import numpy as np
from jax.experimental import pallas as pl
from jax.experimental.pallas import tpu as pltpu

D_MODEL = 2048
BATCH = 2
SEQ = 4096
DEPTH = 2
DEC_BATCH = 32
DEC_SEQ = 64
PAST_LEN = 2048

CHUNK = 64
W_LRU = 1024
LRU_BLOCKS = 16
LRU_CONV = 4
LRU_C = 8.0
W_CONF = 1024
CONF_CONV = 31
N_HEADS = 8
N_KV = 2
HEAD_DIM = 128
ROT_DIM = HEAD_DIM // 4
N_IDX_HEADS = 8
IDX_DIM = 64
IDX_ROT = IDX_DIM // 4
TOPK_ATT = 256
ROPE_THETA = 500000.0
Q_BLOCK = 128
W_POOL = 1024
POOL_WINDOWS = (2, 4, 8, 16)
POOL_GROUPS = 4
POOL_GW = W_POOL // POOL_GROUPS
POOL_BUF = 15
N_BRANCH = 4
W_BRANCH = 1024
PEER_HEADS = 8
PEER_NKEYS = 128
PEER_TOPK = 16
PEER_EXPERTS = PEER_NKEYS * PEER_NKEYS
PEER_QDIM = 256
PEER_HALF = PEER_QDIM // 2
PEER_BLOCK = 128
ALPHA = (2 * DEPTH) ** 0.25
BETA = (8 * DEPTH) ** -0.25
LN_EPS = 1e-5
PROJ_SIZES = (W_LRU, W_LRU, 2 * W_CONF, N_HEADS * HEAD_DIM, N_KV * HEAD_DIM, N_KV * HEAD_DIM,
              N_IDX_HEADS * IDX_DIM, IDX_DIM, N_IDX_HEADS, W_POOL, N_BRANCH * D_MODEL)
PROJ_TOTAL = 15432

F32 = jnp.float32
BF16 = jnp.bfloat16
VMEM_LIMIT_BYTES = 56 * 1024 * 1024
LANES = 128
GELU_C = math.sqrt(2.0 / math.pi)
I32 = jnp.int32
CHUNK_BITS = 6
INT_MIN = -2 ** 31
NEG_BIG = -1e30


def _cparams(*sem):
    return pltpu.CompilerParams(dimension_semantics=sem, vmem_limit_bytes=VMEM_LIMIT_BYTES)


def _peer_scores_body(x_ref, wq_ref, sk_ref, s_ref, xt_ref, *, n_hc):
    x = x_ref[...]
    xt_ref[...] = x.T.astype(BF16)
    q = jnp.dot(x.astype(BF16), wq_ref[...], preferred_element_type=F32)
    for hc in range(n_hc):
        qh = q[:, hc * LANES:(hc + 1) * LANES].astype(BF16)
        s_ref[hc] = lax.dot_general(sk_ref[hc], qh, (((1,), (1,)), ((), ())), preferred_element_type=F32)


def peer_scores(x, wq_bf, sk_bf, tb):
    m, d = x.shape
    n_hc = sk_bf.shape[0]
    return pl.pallas_call(
        functools.partial(_peer_scores_body, n_hc=n_hc),
        grid=(m // tb,),
        in_specs=[pl.BlockSpec((tb, d), lambda i: (i, 0)),
                  pl.BlockSpec((d, n_hc * LANES), lambda i: (0, 0)),
                  pl.BlockSpec((n_hc, LANES, LANES), lambda i: (0, 0, 0))],
        out_specs=[pl.BlockSpec((n_hc, LANES, tb), lambda i: (0, 0, i)),
                   pl.BlockSpec((d, tb), lambda i: (0, i))],
        out_shape=[jax.ShapeDtypeStruct((n_hc, LANES, m), F32),
                   jax.ShapeDtypeStruct((d, m), BF16)],
        compiler_params=_cparams("arbitrary"),
    )(x, wq_bf, sk_bf)


N_EXTRACT = PEER_TOPK + 1


def _top_desc(s, n):
    rows = lax.broadcasted_iota(jnp.int32, s.shape, 0)
    rem = s
    out = []
    for k in range(n):
        mx = jnp.max(rem, axis=0, keepdims=True)
        out.append(mx)
        if k + 1 < n:
            first = jnp.min(jnp.where(rem == mx, rows, PEER_NKEYS), axis=0, keepdims=True)
            rem = jnp.where(rows == first, -jnp.inf, rem)
    return out


def _stack_rows(vals, nrows, tl):
    rows = lax.broadcasted_iota(jnp.int32, (nrows, tl), 0)
    acc = jnp.full((nrows, tl), -jnp.inf, F32)
    for k, v in enumerate(vals):
        acc = jnp.where(rows == k, v, acc)
    return acc


def _peer_select_body(s_ref, a_ref, b_ref, c_ref, *, n_heads, tl):
    n = N_EXTRACT
    rows8 = lax.broadcasted_iota(jnp.int32, (8, tl), 0)
    for h in range(n_heads):
        s1 = s_ref[2 * h]
        s2 = s_ref[2 * h + 1]
        sv1 = _top_desc(s1, n)
        sv2 = _top_desc(s2, n)
        pieces = [_stack_rows(sv2, 24, tl) + sv1[0]]
        sv2_8 = _stack_rows(sv2[:8], 8, tl)
        for i in range(1, 8):
            nj = n // (i + 1)
            pieces.append(jnp.where(rows8 < nj, sv2_8 + sv1[i], -jnp.inf))
        pieces.append(_stack_rows(sv1[8:], 16, tl) + sv2[0])
        cand = jnp.concatenate(pieces, axis=0)
        taken = jnp.zeros((1, tl), F32)
        v16 = jnp.zeros((1, tl), F32)
        v17 = jnp.zeros((1, tl), F32)
        rem = cand
        for k in range(n):
            mx = jnp.max(rem, axis=0, keepdims=True)
            eq = rem == mx
            cnt = jnp.sum(jnp.where(eq, 1.0, 0.0), axis=0, keepdims=True)
            new_taken = taken + cnt
            v16 = jnp.where((taken < PEER_TOPK) & (new_taken >= PEER_TOPK), mx, v16)
            v17 = jnp.where((taken < n) & (new_taken >= n), mx, v17)
            rem = jnp.where(eq, -jnp.inf, rem)
            taken = new_taken
        thr = 0.5 * (v16 + v17)
        top = sv1[0] + sv2[0]
        z = jnp.sum(jnp.where(cand >= v16, jnp.exp(cand - top), 0.0), axis=0, keepdims=True)
        a_ref[h] = jnp.exp(s1 - sv1[0]) / z
        b_ref[h] = jnp.exp(s2 - sv2[0])
        c_ref[h] = thr - s1


def peer_select(s, tl):
    n_hc, _, m = s.shape
    n_heads = n_hc // 2
    spec = pl.BlockSpec((n_heads, PEER_NKEYS, tl), lambda i: (0, 0, i))
    shp = jax.ShapeDtypeStruct((n_heads, PEER_NKEYS, m), F32)
    return pl.pallas_call(
        functools.partial(_peer_select_body, n_heads=n_heads, tl=tl),
        grid=(m // tl,),
        in_specs=[pl.BlockSpec((n_hc, PEER_NKEYS, tl), lambda i: (0, 0, i))],
        out_specs=[spec, spec, spec],
        out_shape=[shp, shp, shp],
        compiler_params=_cparams("arbitrary"),
    )(s)


def _gelu_tanh(x):
    return 0.5 * x * (1.0 + jnp.tanh(GELU_C * (x + 0.044715 * (x * x * x))))


def _peer_dense_body(xt_ref, u_ref, vt_ref, a_ref, b_ref, c_ref, s2_ref, o_ref, acc_ref, act_ref, coef_ref,
                     *, n_heads, eb, tbt, lc):
    j = pl.program_id(1)

    @pl.when(j == 0)
    def _():
        acc_ref[...] = jnp.zeros_like(acc_ref)

    act_ref[...] = jnp.dot(u_ref[...], xt_ref[...], preferred_element_type=F32)
    nr = eb // PEER_NKEYS

    def lane_chunk(ci, carry):
        l0 = pl.multiple_of(ci * lc, lc)
        for rr in range(nr):
            r = j * nr + rr
            w = jnp.zeros((PEER_NKEYS, lc), F32)
            for h in range(n_heads):
                a = a_ref[h, pl.ds(r, 1), pl.ds(l0, lc)]
                c = c_ref[h, pl.ds(r, 1), pl.ds(l0, lc)]
                s2 = s2_ref[h, :, pl.ds(l0, lc)]
                b = b_ref[h, :, pl.ds(l0, lc)]
                w = w + jnp.where(s2 >= c, a * b, 0.0)
            x = act_ref[rr * PEER_NKEYS:(rr + 1) * PEER_NKEYS, pl.ds(l0, lc)]
            coef_ref[rr * PEER_NKEYS:(rr + 1) * PEER_NKEYS, pl.ds(l0, lc)] = (w * _gelu_tanh(x)).astype(BF16)
        return carry

    lax.fori_loop(0, tbt // lc, lane_chunk, 0)
    acc_ref[...] += jnp.dot(vt_ref[...], coef_ref[...], preferred_element_type=F32)

    @pl.when(j == pl.num_programs(1) - 1)
    def _():
        o_ref[...] = acc_ref[...].T


def peer_dense(xt_bf, u_bf, vt_bf, a, b, c, s, tbt, eb, lc=256):
    d, m = xt_bf.shape
    e = u_bf.shape[0]
    n_heads = a.shape[0]
    s4 = s.reshape(n_heads, 2, PEER_NKEYS, m)
    hspec = pl.BlockSpec((n_heads, PEER_NKEYS, tbt), lambda i, j: (0, 0, i))
    return pl.pallas_call(
        functools.partial(_peer_dense_body, n_heads=n_heads, eb=eb, tbt=tbt, lc=lc),
        grid=(m // tbt, e // eb),
        in_specs=[pl.BlockSpec((d, tbt), lambda i, j: (0, i)),
                  pl.BlockSpec((eb, d), lambda i, j: (j, 0)),
                  pl.BlockSpec((d, eb), lambda i, j: (0, j)),
                  hspec, hspec, hspec,
                  pl.BlockSpec((n_heads, None, PEER_NKEYS, tbt), lambda i, j: (0, 1, 0, i))],
        out_specs=pl.BlockSpec((tbt, d), lambda i, j: (i, 0)),
        out_shape=jax.ShapeDtypeStruct((m, d), F32),
        scratch_shapes=[pltpu.VMEM((d, tbt), F32), pltpu.VMEM((eb, tbt), F32), pltpu.VMEM((eb, tbt), BF16)],
        compiler_params=_cparams("arbitrary", "arbitrary"),
    )(xt_bf, u_bf, vt_bf, a, b, c, s4)


def peer_pallas(x, wq, sub_keys, u_tab, v_tab, tb=512, tl=256, tbt=512, eb=512):
    n_heads = sub_keys.shape[0]
    sk_bf = sub_keys.reshape(n_heads * 2, PEER_NKEYS, -1).astype(BF16)
    s, xt = peer_scores(x, wq.astype(BF16), sk_bf, tb)
    a, b, c = peer_select(s, tl)
    return peer_dense(xt, u_tab.astype(BF16), v_tab.T.astype(BF16), a, b, c, s, tbt, eb)


def rope_tables(pos):
    posf = pos.astype(F32)[:, None]

    def table(rot, period, lanes_active):
        half = rot // 2
        inv = ROPE_THETA ** (-jnp.arange(half, dtype=F32) / half)
        ang = posf * inv[None, :]
        cos, sin = jnp.cos(ang), jnp.sin(ang)
        m = pos.shape[0]
        c = jnp.concatenate([cos, cos, jnp.ones((m, period - rot), F32)], axis=1)
        s = jnp.concatenate([-sin, sin, jnp.zeros((m, period - rot), F32)], axis=1)
        reps = LANES // period
        c, s = jnp.tile(c, (1, reps)), jnp.tile(s, (1, reps))
        lane = jnp.arange(LANES)[None, :]
        c = jnp.where(lane < lanes_active, c, 1.0)
        s = jnp.where(lane < lanes_active, s, 0.0)
        return c, s

    ch, sh = table(ROT_DIM, HEAD_DIM, LANES)
    ci, si = table(IDX_ROT, IDX_DIM, LANES)
    ck, sk = table(IDX_ROT, IDX_DIM, IDX_DIM)
    return ch, sh, ci, si, ck, sk


def _swap_halves(x, half, period):
    lane = lax.broadcasted_iota(I32, x.shape, 1)
    return jnp.where((lane % period) < half, pltpu.roll(x, LANES - half, 1), pltpu.roll(x, half, 1))


def _dsa_prep_body(q_ref, k_ref, v_ref, qi_ref, sm_ref, ch_ref, sh_ref, ci_ref, si_ref, ck_ref, sk_ref,
                   qo_ref, ko_ref, kb_ref, vb_ref, qio_ref, smo_ref, kib_ref):
    ch, sh = ch_ref[...], sh_ref[...]
    for h in range(N_HEADS):
        x = q_ref[:, h * LANES:(h + 1) * LANES]
        qo_ref[:, h * LANES:(h + 1) * LANES] = (x * ch + _swap_halves(x, ROT_DIM // 2, HEAD_DIM) * sh).astype(BF16)
    for h in range(N_KV):
        x = k_ref[:, h * LANES:(h + 1) * LANES]
        kr = x * ch + _swap_halves(x, ROT_DIM // 2, HEAD_DIM) * sh
        ko_ref[:, h * LANES:(h + 1) * LANES] = kr
        kb_ref[:, h * LANES:(h + 1) * LANES] = kr.astype(BF16)
    vb_ref[...] = v_ref[...].astype(BF16)
    ci, si = ci_ref[...], si_ref[...]
    for p in range(N_IDX_HEADS * IDX_DIM // LANES):
        x = qi_ref[:, p * LANES:(p + 1) * LANES]
        qio_ref[:, p * LANES:(p + 1) * LANES] = (x * ci + _swap_halves(x, IDX_ROT // 2, IDX_DIM) * si).astype(BF16)
    x = sm_ref[...]
    sm = x * ck_ref[...] + _swap_halves(x, IDX_ROT // 2, IDX_DIM) * sk_ref[...]
    smo_ref[...] = sm
    kib_ref[...] = sm.astype(BF16)


def dsa_prep(z, tables, tb, col):
    m = z.shape[0]

    def zspec(width, start):
        return pl.BlockSpec((tb, width), lambda i: (i, start // width))

    def ospec(width):
        return pl.BlockSpec((tb, width), lambda i: (i, 0))

    tspec = pl.BlockSpec((tb, LANES), lambda i: (i, 0))
    nq, nkv, nqi = N_HEADS * HEAD_DIM, N_KV * HEAD_DIM, N_IDX_HEADS * IDX_DIM
    return pl.pallas_call(
        _dsa_prep_body,
        grid=(m // tb,),
        in_specs=[zspec(nq, col['q']), zspec(nkv, col['k']), zspec(nkv, col['v']), zspec(nqi, col['qi']),
                  zspec(LANES, col['small'])] + [tspec] * 6,
        out_specs=[ospec(nq), ospec(nkv), ospec(nkv), ospec(nkv), ospec(nqi), ospec(LANES), ospec(LANES)],
        out_shape=[jax.ShapeDtypeStruct((m, nq), BF16), jax.ShapeDtypeStruct((m, nkv), F32),
                   jax.ShapeDtypeStruct((m, nkv), BF16), jax.ShapeDtypeStruct((m, nkv), BF16),
                   jax.ShapeDtypeStruct((m, nqi), BF16), jax.ShapeDtypeStruct((m, LANES), F32),
                   jax.ShapeDtypeStruct((m, LANES), BF16)],
        compiler_params=_cparams("arbitrary"),
    )(z, z, z, z, z, *tables)


def _sortable(x):
    b = lax.bitcast_convert_type(x + 0.0, I32)
    return b ^ ((b >> 31) & 0x7FFFFFFF)


def _index_queries(qi, sm):
    lane = lax.broadcasted_iota(I32, (qi.shape[0], LANES), 1)
    qs, ws = [], []
    for h in range(N_IDX_HEADS):
        x = qi[:, (h // 2) * LANES:(h // 2 + 1) * LANES].astype(F32)
        if h % 2 == 1:
            x = pltpu.roll(x, IDX_DIM, 1)
        qs.append(jnp.where(lane < IDX_DIM, x, 0.0).astype(BF16))
        ws.append(sm[:, IDX_DIM + h:IDX_DIM + h + 1])
    return qs, ws


def _index_scores(qs, ws, ki_tile):
    score = None
    for q, w in zip(qs, ws):
        d = lax.dot_general(q, ki_tile, (((1,), (1,)), ((), ())), preferred_element_type=F32)
        t = w * jnp.maximum(d, 0.0)
        score = t if score is None else score + t
    return score


def _count(keys_ref, n_tiles, kt, rows, pred):
    def body(t, acc):
        base = pl.multiple_of(t * kt, kt)
        for c in range(kt // LANES):
            tile = keys_ref[:, pl.ds(base + c * LANES, LANES)]
            idx = base + c * LANES + lax.broadcasted_iota(I32, (rows, LANES), 1)
            acc = acc + jnp.where(pred(tile, idx), 1.0, 0.0)
        return acc

    acc = lax.fori_loop(0, n_tiles, body, jnp.zeros((rows, LANES), F32))
    return jnp.sum(acc, axis=1, keepdims=True)


def _select_threshold(keys_ref, n_tiles, kt, rows, n_sel, n_lane_bits, j_ref):
    def bcast(v):
        return jnp.broadcast_to(v, (rows, LANES))

    ge0 = _count(keys_ref, n_tiles, kt, rows, lambda k, i: k >= 0)
    lo0 = jnp.where(ge0 >= n_sel, 0, INT_MIN).astype(I32)

    def bit_step(i, lo):
        cand = lo + jnp.left_shift(jnp.int32(1), 30 - i)
        cb = bcast(cand)
        cnt = _count(keys_ref, n_tiles, kt, rows, lambda k, idx: k >= cb)
        return jnp.where(cnt >= n_sel, cand, lo)

    tau = lax.fori_loop(0, 31, bit_step, lo0)
    tb = bcast(tau)
    n_ge = _count(keys_ref, n_tiles, kt, rows, lambda k, i: k >= tb)
    n_lanes_max = 1 << n_lane_bits
    j_ref[...] = jnp.full((rows, 1), n_lanes_max, I32)

    @pl.when(jnp.max(n_ge - n_sel) > 0.0)
    def _():
        n_gt = _count(keys_ref, n_tiles, kt, rows, lambda k, i: k > tb)
        need = n_sel - n_gt

        def idx_step(i, j):
            cand = j + jnp.left_shift(jnp.int32(1), n_lane_bits - i)
            cb = bcast(cand)
            cnt = _count(keys_ref, n_tiles, kt, rows, lambda k, idx: (k == tb) & (idx < cb))
            return jnp.where((cand <= n_lanes_max) & (cnt <= need), cand, j)

        j_ref[...] = lax.fori_loop(0, n_lane_bits + 1, idx_step, jnp.zeros((rows, 1), I32))

    return tau


def _attend_tile(qg, k_tile, v_tile, sel, m_ref, l_ref, acc_ref):
    per = N_HEADS // N_KV
    tq, w = sel.shape
    scale = HEAD_DIM ** -0.5
    for g in range(N_KV):
        s = lax.dot_general(qg[g], k_tile[:, g * HEAD_DIM:(g + 1) * HEAD_DIM], (((1,), (1,)), ((), ())),
                            preferred_element_type=F32) * scale
        s = jnp.where(sel[None], s.reshape(per, tq, w), NEG_BIG)
        m_old = m_ref[g]
        m_new = jnp.maximum(m_old, jnp.max(s, axis=2, keepdims=True))
        alpha = jnp.exp(m_old - m_new)
        p = jnp.where(sel[None], jnp.exp(s - m_new), 0.0)
        l_ref[g] = alpha * l_ref[g] + jnp.sum(p, axis=2, keepdims=True)
        pv = jnp.dot(p.reshape(per * tq, w).astype(BF16), v_tile[:, g * HEAD_DIM:(g + 1) * HEAD_DIM],
                     preferred_element_type=F32)
        acc_ref[g] = alpha * acc_ref[g] + pv.reshape(per, tq, HEAD_DIM)
        m_ref[g] = m_new


def _query_stacks(q):
    per = N_HEADS // N_KV
    return [jnp.concatenate([q[:, (g * per + i) * HEAD_DIM:(g * per + i + 1) * HEAD_DIM] for i in range(per)], axis=0)
            for g in range(N_KV)]


def _init_softmax(m_ref, l_ref, acc_ref):
    m_ref[...] = jnp.full(m_ref.shape, NEG_BIG, F32)
    l_ref[...] = jnp.zeros(l_ref.shape, F32)
    acc_ref[...] = jnp.zeros(acc_ref.shape, F32)


def _write_heads(o_ref, l_ref, acc_ref, tq):
    per = N_HEADS // N_KV
    for g in range(N_KV):
        o = acc_ref[g] / l_ref[g]
        for i in range(per):
            h = g * per + i
            o_ref[:, h * HEAD_DIM:(h + 1) * HEAD_DIM] = o[i].astype(o_ref.dtype)


def _selected(keys, idx, tau, j):
    return (keys > tau) | ((keys == tau) & (idx < j))


def _dsa_prompt_body(q_ref, qi_ref, sm_ref, k_ref, v_ref, ki_ref, o_ref,
                     keys_ref, j_ref, m_ref, l_ref, acc_ref, *, tq, kt, n_lane_bits):
    jq = pl.program_id(1)
    p0 = jq * tq
    row = lax.broadcasted_iota(I32, (tq, 1), 0)
    limit = p0 + (jnp.right_shift(row, CHUNK_BITS) + 1) * CHUNK
    n_tiles = (p0 + tq + kt - 1) // kt
    qs, ws = _index_queries(qi_ref[...], sm_ref[...])

    def score_tile(t, carry):
        base = pl.multiple_of(t * kt, kt)
        sc = _index_scores(qs, ws, ki_ref[pl.ds(base, kt), :])
        idx = base + lax.broadcasted_iota(I32, (tq, kt), 1)
        keys_ref[:, pl.ds(base, kt)] = jnp.where(idx < limit, _sortable(sc), INT_MIN)
        return carry

    lax.fori_loop(0, n_tiles, score_tile, 0)
    n_sel = jnp.minimum(limit, TOPK_ATT).astype(F32)
    tau = _select_threshold(keys_ref, n_tiles, kt, tq, n_sel, n_lane_bits, j_ref)
    jb = j_ref[...]
    qg = _query_stacks(q_ref[...])
    _init_softmax(m_ref, l_ref, acc_ref)

    def attend(t, carry):
        base = pl.multiple_of(t * kt, kt)
        idx = base + lax.broadcasted_iota(I32, (tq, kt), 1)
        sel = _selected(keys_ref[:, pl.ds(base, kt)], idx, tau, jb)
        _attend_tile(qg, k_ref[pl.ds(base, kt), :], v_ref[pl.ds(base, kt), :], sel, m_ref, l_ref, acc_ref)
        return carry

    lax.fori_loop(0, n_tiles, attend, 0)
    _write_heads(o_ref, l_ref, acc_ref, tq)


def dsa_prompt(q_bf, qi_bf, sm, k_bf, v_bf, ki_bf, bsz, t, tq, kt):
    nq = t // tq
    n_lane_bits = int(math.log2(t))
    nqd, nkv, nqi = N_HEADS * HEAD_DIM, N_KV * HEAD_DIM, N_IDX_HEADS * IDX_DIM
    per = N_HEADS // N_KV

    def qspec(w):
        return pl.BlockSpec((tq, w), lambda b, j: (b * nq + j, 0))

    def kspec(w):
        return pl.BlockSpec((t, w), lambda b, j: (b, 0))

    return pl.pallas_call(
        functools.partial(_dsa_prompt_body, tq=tq, kt=kt, n_lane_bits=n_lane_bits),
        grid=(bsz, nq),
        in_specs=[qspec(nqd), qspec(nqi), qspec(LANES), kspec(nkv), kspec(nkv), kspec(LANES)],
        out_specs=qspec(nqd),
        out_shape=jax.ShapeDtypeStruct((bsz * t, nqd), BF16),
        scratch_shapes=[pltpu.VMEM((tq, t), I32), pltpu.VMEM((tq, 1), I32),
                        pltpu.VMEM((N_KV, per, tq, 1), F32), pltpu.VMEM((N_KV, per, tq, 1), F32),
                        pltpu.VMEM((N_KV, per, tq, HEAD_DIM), F32)],
        compiler_params=_cparams("arbitrary", "arbitrary"),
    )(q_bf, qi_bf, sm, k_bf, v_bf, ki_bf)


def _dsa_sample_body(q_ref, qi_ref, sm_ref, kn_ref, vn_ref, kin_ref, ck_ref, cv_ref, cki_ref, o_ref,
                     keys_ref, j_ref, m_ref, l_ref, acc_ref, *, tq, kt, past, n_lane_bits):
    n_cache_tiles = past // kt
    n_new = LANES
    n_keys = past + tq
    lanes_total = past + n_new
    qs, ws = _index_queries(qi_ref[...], sm_ref[...])

    def pad_rows(x):
        return jnp.concatenate([x, jnp.zeros((n_new - tq, x.shape[1]), x.dtype)], axis=0)

    for t in range(n_cache_tiles):
        sc = _index_scores(qs, ws, cki_ref[t * kt:(t + 1) * kt, :])
        keys_ref[:, t * kt:(t + 1) * kt] = _sortable(sc)
    sc = _index_scores(qs, ws, pad_rows(kin_ref[...]))
    idx_new = past + lax.broadcasted_iota(I32, (tq, n_new), 1)
    keys_ref[:, past:lanes_total] = jnp.where(idx_new < n_keys, _sortable(sc), INT_MIN)

    n_sel = jnp.full((tq, 1), float(min(TOPK_ATT, n_keys // 4)), F32)
    tau = _select_threshold(keys_ref, lanes_total // LANES, LANES, tq, n_sel, n_lane_bits, j_ref)
    jb = j_ref[...]
    qg = _query_stacks(q_ref[...])
    _init_softmax(m_ref, l_ref, acc_ref)
    for t in range(n_cache_tiles):
        idx = t * kt + lax.broadcasted_iota(I32, (tq, kt), 1)
        sel = _selected(keys_ref[:, t * kt:(t + 1) * kt], idx, tau, jb)
        _attend_tile(qg, ck_ref[t * kt:(t + 1) * kt, :].astype(BF16), cv_ref[t * kt:(t + 1) * kt, :].astype(BF16),
                     sel, m_ref, l_ref, acc_ref)
    sel = _selected(keys_ref[:, past:lanes_total], idx_new, tau, jb)
    _attend_tile(qg, pad_rows(kn_ref[...]), pad_rows(vn_ref[...]), sel, m_ref, l_ref, acc_ref)
    _write_heads(o_ref, l_ref, acc_ref, tq)


def dsa_sample(q_bf, qi_bf, sm, k_bf, v_bf, ki_bf, cache_k, cache_v, cache_ki_bf, row0, bsz, tq, kt):
    past = cache_k.shape[1]
    n_lane_bits = int(math.ceil(math.log2(past + LANES)))
    nqd, nkv, nqi = N_HEADS * HEAD_DIM, N_KV * HEAD_DIM, N_IDX_HEADS * IDX_DIM
    per = N_HEADS // N_KV
    b0 = row0 // tq

    def qspec(w):
        return pl.BlockSpec((tq, w), lambda b: (b0 + b, 0))

    def cspec(w):
        return pl.BlockSpec((None, past, w), lambda b: (b, 0, 0))

    return pl.pallas_call(
        functools.partial(_dsa_sample_body, tq=tq, kt=kt, past=past, n_lane_bits=n_lane_bits),
        grid=(bsz,),
        in_specs=[qspec(nqd), qspec(nqi), qspec(LANES), qspec(nkv), qspec(nkv), qspec(LANES),
                  cspec(nkv), cspec(nkv), cspec(LANES)],
        out_specs=pl.BlockSpec((tq, nqd), lambda b: (b, 0)),
        out_shape=jax.ShapeDtypeStruct((bsz * tq, nqd), BF16),
        scratch_shapes=[pltpu.VMEM((tq, past + LANES), I32), pltpu.VMEM((tq, 1), I32),
                        pltpu.VMEM((N_KV, per, tq, 1), F32), pltpu.VMEM((N_KV, per, tq, 1), F32),
                        pltpu.VMEM((N_KV, per, tq, HEAD_DIM), F32)],
        compiler_params=_cparams("arbitrary"),
    )(q_bf, qi_bf, sm, k_bf, v_bf, ki_bf, cache_k, cache_v, cache_ki_bf)


def _layer_norm(x, g, b):
    xf = x.astype(jnp.float32)
    mu = jnp.mean(xf, -1, keepdims=True)
    var = jnp.mean(jnp.square(xf - mu), -1, keepdims=True)
    y = (xf - mu) * lax.rsqrt(var + LN_EPS) * g.astype(jnp.float32) + b.astype(jnp.float32)
    return y.astype(x.dtype)


def _rope_partial(x, pos, rot):
    half = rot // 2
    inv = ROPE_THETA ** (-jnp.arange(half, dtype=jnp.float32) / half)
    ang = pos.astype(jnp.float32)[:, None] * inv[None, :]
    cos = jnp.cos(ang)[None, :, None, :]
    sin = jnp.sin(ang)[None, :, None, :]
    xr = x[..., :rot].astype(jnp.float32)
    x1, x2 = xr[..., :half], xr[..., half:]
    rot_part = jnp.concatenate([x1 * cos - x2 * sin, x2 * cos + x1 * sin], -1)
    return jnp.concatenate([rot_part.astype(x.dtype), x[..., rot:]], -1)


def _causal_dwconv(buf, x, w, b):
    t = x.shape[1]
    xp = jnp.concatenate([buf.astype(x.dtype), x], axis=1)
    y = lax.conv_general_dilated(xp, w.astype(x.dtype)[:, None, :], (1,), 'VALID',
                                 dimension_numbers=('NWC', 'WIO', 'NWC'),
                                 feature_group_count=x.shape[-1])
    return y + b.astype(x.dtype), xp[:, t:]


def _lin_combine(left, right):
    a_l, b_l = left
    a_r, b_r = right
    return a_l * a_r, a_r * b_l + b_r


def _rglru(xc, h0, w_r, b_r, w_i, b_i, lam):
    bsz, t, c = xc.shape
    xb = xc.reshape(bsz, t, LRU_BLOCKS, c // LRU_BLOCKS)
    gr = jnp.einsum('btnc,ncd->btnd', xb, w_r).reshape(bsz, t, c) + b_r
    gi = jnp.einsum('btnc,ncd->btnd', xb, w_i).reshape(bsz, t, c) + b_i
    r = jax.nn.sigmoid(gr.astype(jnp.float32))
    i = jax.nn.sigmoid(gi.astype(jnp.float32))
    log_a = -LRU_C * r * jax.nn.softplus(-lam.astype(jnp.float32))
    a = jnp.exp(log_a)
    u = jnp.sqrt(-jnp.expm1(2.0 * log_a)) * i * xc.astype(jnp.float32)
    u = u.at[:, 0].add(a[:, 0] * h0.astype(jnp.float32))
    _, h = lax.associative_scan(_lin_combine, (a, u), axis=1)
    return h, h[:, -1]


def _pool_mix(buf, xin, w_pool, scale):
    bsz, t, c = xin.shape
    xp = jnp.concatenate([buf.astype(xin.dtype), xin], axis=1)
    xf = xp.astype(jnp.float32)
    cs = jnp.concatenate([jnp.zeros((bsz, 1, c), jnp.float32), jnp.cumsum(xf, axis=1)], axis=1)
    means = []
    for g, w in enumerate(POOL_WINDOWS):
        lo = g * POOL_GW
        hi = lo + POOL_GW
        win_sum = cs[:, POOL_BUF + 1:, lo:hi] - cs[:, POOL_BUF + 1 - w:POOL_BUF + 1 - w + t, lo:hi]
        means.append(win_sum * (1.0 / w))
    diff = jnp.concatenate(means, -1) - xf[:, POOL_BUF:]
    diff = diff.astype(xin.dtype).reshape(bsz, t, POOL_GROUPS, POOL_GW)
    y = jnp.einsum('btgc,gcd->btgd', diff, w_pool).reshape(bsz, t, c) * scale
    return y, xp[:, t:]


def _dsa_block(q, qi, wi, q_pos, k_all, v_all, ki_all, n_top):
    bsz, tq = q.shape[0], q.shape[1]
    n_keys = k_all.shape[1]
    limit = (q_pos // CHUNK + 1) * CHUNK
    adm = jnp.arange(n_keys, dtype=jnp.int32)[None, :] < limit[:, None]
    dots = jnp.einsum('bqhd,bsd->bqhs', qi, ki_all).astype(jnp.float32)
    score = jnp.einsum('bqh,bqhs->bqs', wi.astype(jnp.float32), jax.nn.relu(dots))
    score = jnp.where(adm[None], score, -jnp.inf)
    _, idx = lax.top_k(score, n_top)
    valid = idx < limit[None, :, None]
    gather = jax.vmap(lambda rows, ids: rows[ids])
    kg = gather(k_all, idx)
    vg = gather(v_all, idx)
    qg = q.reshape(bsz, tq, N_KV, N_HEADS // N_KV, HEAD_DIM)
    s = jnp.einsum('bqgrd,bqkgd->bqgrk', qg, kg).astype(jnp.float32) * (HEAD_DIM ** -0.5)
    s = jnp.where(valid[:, :, None, None, :], s, -jnp.inf)
    p = jax.nn.softmax(s, axis=-1).astype(vg.dtype)
    o = jnp.einsum('bqgrk,bqkgd->bqgrd', p, vg)
    return o.reshape(bsz, tq, N_HEADS * HEAD_DIM)


def _dsa_attention(q, qi, wi, pos, k_all, v_all, ki_all):
    bsz, t = q.shape[0], q.shape[1]
    n_top = min(TOPK_ATT, k_all.shape[1] // 4)
    if t > Q_BLOCK and t % Q_BLOCK == 0:
        nb = t // Q_BLOCK

        def to_blocks(a):
            return jnp.moveaxis(a.reshape((bsz, nb, Q_BLOCK) + a.shape[2:]), 1, 0)

        def one_block(args):
            qb, qib, wib, pb = args
            return _dsa_block(qb, qib, wib, pb, k_all, v_all, ki_all, n_top)

        out = lax.map(one_block, (to_blocks(q), to_blocks(qi), to_blocks(wi), pos.reshape(nb, Q_BLOCK)))
        return jnp.moveaxis(out, 0, 1).reshape(bsz, t, N_HEADS * HEAD_DIM)
    return _dsa_block(q, qi, wi, pos, k_all, v_all, ki_all, n_top)


def _mm_body(x_ref, w_ref, o_ref):
    o_ref[...] = jnp.dot(x_ref[...], w_ref[...], preferred_element_type=F32)


def matmul_bf16(x_bf, w_bf, tm, tn):
    m, k = x_bf.shape
    n = w_bf.shape[1]
    return pl.pallas_call(
        _mm_body,
        grid=(m // tm, n // tn),
        in_specs=[pl.BlockSpec((tm, k), lambda i, j: (i, 0)),
                  pl.BlockSpec((k, tn), lambda i, j: (0, j))],
        out_specs=pl.BlockSpec((tm, tn), lambda i, j: (i, j)),
        out_shape=jax.ShapeDtypeStruct((m, n), F32),
        compiler_params=_cparams("arbitrary", "arbitrary"),
    )(x_bf, w_bf)


def _res_ln_body(x_ref, y_ref, g_ref, b_ref, o_ref, ob_ref):
    r = ALPHA * x_ref[...] + y_ref[...]
    mu = jnp.mean(r, axis=-1, keepdims=True)
    rc = r - mu
    var = jnp.mean(rc * rc, axis=-1, keepdims=True)
    out = rc * lax.rsqrt(var + LN_EPS) * g_ref[...] + b_ref[...]
    o_ref[...] = out
    ob_ref[...] = out.astype(BF16)


def residual_layer_norm(x, y, g, b, tb):
    m, d = x.shape
    row = pl.BlockSpec((tb, d), lambda i: (i, 0))
    vec = pl.BlockSpec((1, d), lambda i: (0, 0))
    return pl.pallas_call(
        _res_ln_body,
        grid=(m // tb,),
        in_specs=[row, row, vec, vec],
        out_specs=[row, row],
        out_shape=[jax.ShapeDtypeStruct((m, d), F32), jax.ShapeDtypeStruct((m, d), BF16)],
        compiler_params=_cparams("arbitrary"),
    )(x, y, g.reshape(1, d), b.reshape(1, d))


N_PROMPT = BATCH * SEQ
N_SAMPLE = DEC_BATCH * DEC_SEQ
IN_OFFS = tuple(int(o) for o in np.cumsum(PROJ_SIZES)[:-1])
IN_ALIGNED = IN_OFFS[6]
IN_SMALL = PROJ_SIZES[7] + PROJ_SIZES[8]
IN_TN = 1408
POOL_COL = IN_ALIGNED
GATE_COL = POOL_COL + W_POOL
SMALL_COL = GATE_COL + N_BRANCH * D_MODEL
DSA_COLS = dict(q=IN_OFFS[2], k=IN_OFFS[3], v=IN_OFFS[4], qi=IN_OFFS[5], small=SMALL_COL)
DSA_PREP_TB = 512
DSA_KEY_TILE = 512
MERGE_TB = 512
MERGE_TN = 512


def _in_proj_weight(w_in_l):
    small = jnp.pad(w_in_l[:, IN_ALIGNED:IN_ALIGNED + IN_SMALL], ((0, 0), (0, LANES - IN_SMALL)))
    return jnp.concatenate([w_in_l[:, :IN_ALIGNED], w_in_l[:, IN_ALIGNED + IN_SMALL:], small], axis=1).astype(BF16)


def _merge_body(ya_ref, yb_ref, yc_ref, yd_ref, wb_ref, g0_ref, g1_ref, g2_ref, g3_ref, o_ref):
    acc = None
    for n, (y_ref, g_ref) in enumerate(((ya_ref, g0_ref), (yb_ref, g1_ref), (yc_ref, g2_ref), (yd_ref, g3_ref))):
        t = jax.nn.sigmoid(g_ref[...]) * jnp.dot(y_ref[...], wb_ref[n], preferred_element_type=F32)
        acc = t if acc is None else acc + t
    o_ref[...] = acc.astype(BF16)


def merge_branches(ya, yb, yc, yd, wb_bf, z, gate_col, tb, tn):
    m = ya.shape[0]
    yspec = pl.BlockSpec((tb, W_BRANCH), lambda i, j: (i, 0))

    def gspec(n):
        return pl.BlockSpec((tb, tn), lambda i, j: (i, (gate_col + n * D_MODEL) // tn + j))

    return pl.pallas_call(
        _merge_body,
        grid=(m // tb, D_MODEL // tn),
        in_specs=[yspec, yspec, yspec, yspec, pl.BlockSpec((N_BRANCH, W_BRANCH, tn), lambda i, j: (0, 0, j)),
                  gspec(0), gspec(1), gspec(2), gspec(3)],
        out_specs=pl.BlockSpec((tb, tn), lambda i, j: (i, j)),
        out_shape=jax.ShapeDtypeStruct((m, D_MODEL), BF16),
        compiler_params=_cparams("arbitrary", "arbitrary"),
    )(ya, yb, yc, yd, wb_bf, z, z, z, z)


def _out_ln_body(x_ref, m_ref, w_ref, g_ref, b_ref, o_ref):
    r = ALPHA * x_ref[...] + jnp.dot(m_ref[...], w_ref[...], preferred_element_type=F32)
    mu = jnp.mean(r, axis=-1, keepdims=True)
    rc = r - mu
    var = jnp.mean(rc * rc, axis=-1, keepdims=True)
    o_ref[...] = rc * lax.rsqrt(var + LN_EPS) * g_ref[...] + b_ref[...]


def out_proj_layer_norm(x, merged_bf, w_out_bf, g, b, tb):
    m, d = x.shape
    row = pl.BlockSpec((tb, d), lambda i: (i, 0))
    vec = pl.BlockSpec((1, d), lambda i: (0, 0))
    return pl.pallas_call(
        _out_ln_body,
        grid=(m // tb,),
        in_specs=[row, row, pl.BlockSpec((d, d), lambda i: (0, 0)), vec, vec],
        out_specs=row,
        out_shape=jax.ShapeDtypeStruct((m, d), F32),
        compiler_params=_cparams("arbitrary"),
    )(x, merged_bf, w_out_bf, g.reshape(1, d), b.reshape(1, d))


def _mixers_jax(z, l, bsz, t, h0, lru_buf, conf_buf, pool_buf, prm):
    z = z.reshape(bsz, t, -1)
    za, zg, zb = z[..., :1024], z[..., 1024:2048], z[..., 2048:4096]
    zp = z[..., POOL_COL:POOL_COL + W_POOL]
    xc, lru_new = _causal_dwconv(lru_buf, za, prm['lru_conv_w'][l], prm['lru_conv_b'][l])
    h, h_last = _rglru(xc, h0, prm['lru_wr'][l], prm['lru_br'][l], prm['lru_wi'][l],
                       prm['lru_bi'][l], prm['lru_lambda'][l])
    y_a = h * jax.nn.gelu(zg)
    glu = zb[..., :W_CONF] * jax.nn.sigmoid(zb[..., W_CONF:])
    cb, conf_new = _causal_dwconv(conf_buf, glu, prm['conf_conv_w'][l], prm['conf_conv_b'][l])
    y_b = jax.nn.silu(_layer_norm(cb, prm['conf_ln_g'][l], prm['conf_ln_b'][l]))
    y_d, pool_new = _pool_mix(pool_buf, zp, prm['pool_w'][l], prm['pool_scale'][l])
    ys = tuple(y.reshape(bsz * t, W_BRANCH).astype(BF16) for y in (y_a, y_b, y_d))
    return ys, (h_last, lru_new, conf_new, pool_new)


def kernel(x_prompt, x_sample, cache_k, cache_v, cache_kidx, state_lru_h, state_lru_conv,
           state_conf_conv, state_pool, w_in, lru_conv_w, lru_conv_b, lru_wr, lru_br, lru_wi,
           lru_bi, lru_lambda, conf_conv_w, conf_conv_b, conf_ln_g, conf_ln_b, pool_w, pool_scale,
           w_branch, w_out, ln1_g, ln1_b, peer_wq, peer_subkeys, peer_u, peer_v, ln2_g, ln2_b):
    prm = {
        'lru_conv_w': lru_conv_w, 'lru_conv_b': lru_conv_b, 'lru_wr': lru_wr,
        'lru_br': lru_br, 'lru_wi': lru_wi, 'lru_bi': lru_bi, 'lru_lambda': lru_lambda,
        'conf_conv_w': conf_conv_w, 'conf_conv_b': conf_conv_b, 'conf_ln_g': conf_ln_g,
        'conf_ln_b': conf_ln_b, 'pool_w': pool_w, 'pool_scale': pool_scale, 'w_branch': w_branch,
        'w_out': w_out,
    }
    dt = x_prompt.dtype
    x = jnp.concatenate([x_prompt.reshape(N_PROMPT, D_MODEL), x_sample.reshape(N_SAMPLE, D_MODEL)], axis=0)
    x_bf = x.astype(BF16)
    zeros_p = dict(
        h=jnp.zeros((BATCH, W_LRU), F32), lru=jnp.zeros((BATCH, LRU_CONV - 1, W_LRU), dt),
        conf=jnp.zeros((BATCH, CONF_CONV - 1, W_CONF), dt), pool=jnp.zeros((BATCH, POOL_BUF, W_POOL), dt))
    pos = jnp.concatenate([jnp.tile(jnp.arange(SEQ, dtype=I32), BATCH),
                           jnp.tile(PAST_LEN + jnp.arange(DEC_SEQ, dtype=I32), DEC_BATCH)])
    tables = rope_tables(pos)
    nkv = N_KV * HEAD_DIM
    st_p, st_s = [], []
    for l in range(DEPTH):
        z = matmul_bf16(x_bf, _in_proj_weight(w_in[l]), 1024, IN_TN)
        q_bf, k_rot, k_bf, v_bf, qi_bf, sm, ki_bf = dsa_prep(z, tables, DSA_PREP_TB, DSA_COLS)
        yc_p = dsa_prompt(q_bf, qi_bf, sm, k_bf, v_bf, ki_bf, BATCH, SEQ, Q_BLOCK, DSA_KEY_TILE)
        cache_ki_bf = jnp.pad(cache_kidx[l], ((0, 0), (0, 0), (0, LANES - IDX_DIM))).astype(BF16)
        yc_s = dsa_sample(q_bf, qi_bf, sm, k_bf, v_bf, ki_bf, cache_k[l].reshape(DEC_BATCH, PAST_LEN, nkv),
                          cache_v[l].reshape(DEC_BATCH, PAST_LEN, nkv), cache_ki_bf, N_PROMPT, DEC_BATCH, DEC_SEQ,
                          DSA_KEY_TILE)
        v_new = z[:, DSA_COLS['v']:DSA_COLS['v'] + nkv]
        ki_new = sm[:, :IDX_DIM]
        ys_p, sp = _mixers_jax(z[:N_PROMPT], l, BATCH, SEQ,
                               zeros_p['h'], zeros_p['lru'], zeros_p['conf'], zeros_p['pool'], prm)
        ys_s, ss = _mixers_jax(z[N_PROMPT:], l, DEC_BATCH, DEC_SEQ, state_lru_h[l], state_lru_conv[l],
                               state_conf_conv[l], state_pool[l], prm)
        y_a, y_b, y_d = (jnp.concatenate([p, s], axis=0) for p, s in zip(ys_p, ys_s))
        y_c = jnp.concatenate([yc_p, yc_s], axis=0)
        sp = (k_rot[:N_PROMPT].reshape(BATCH, SEQ, N_KV, HEAD_DIM), v_new[:N_PROMPT].reshape(BATCH, SEQ, N_KV, HEAD_DIM),
              ki_new[:N_PROMPT].reshape(BATCH, SEQ, IDX_DIM)) + sp
        ss = (k_rot[N_PROMPT:].reshape(DEC_BATCH, DEC_SEQ, N_KV, HEAD_DIM),
              v_new[N_PROMPT:].reshape(DEC_BATCH, DEC_SEQ, N_KV, HEAD_DIM),
              ki_new[N_PROMPT:].reshape(DEC_BATCH, DEC_SEQ, IDX_DIM)) + ss
        merged = merge_branches(y_a, y_b, y_c, y_d, w_branch[l].astype(BF16), z, GATE_COL, MERGE_TB, MERGE_TN)
        x1 = out_proj_layer_norm(x, merged, w_out[l].astype(BF16), ln1_g[l], ln1_b[l], MERGE_TB)
        ffn = peer_pallas(x1, peer_wq[l], peer_subkeys[l], peer_u[l], peer_v[l])
        x, x_bf = residual_layer_norm(x1, ffn, ln2_g[l], ln2_b[l], 512)
        st_p.append(sp)
        st_s.append(ss)
    outs_p = [jnp.stack([s[i] for s in st_p]) for i in range(7)]
    outs_s = [jnp.stack([s[i] for s in st_s]) for i in range(7)]
    y_prompt = x[:N_PROMPT].reshape(BATCH, SEQ, D_MODEL)
    y_sample = x[N_PROMPT:].reshape(DEC_BATCH, DEC_SEQ, D_MODEL)
    return (y_prompt, y_sample, *outs_p, *outs_s)
```
